```python
import math
import jax, jax.numpy as jnp
from jax import lax
import numpy as np

D_MODEL = 4096
BATCH = 4
SEQ = 4096
DEPTH = 1

GRID_W = 64
CTX_LEN = 256
HEAD_DIM = 128
HEADS_A = D_MODEL // 256
KV_HEADS_A = HEADS_A // 4
GROUPS_A = HEADS_A // KV_HEADS_A
HEADS_B = D_MODEL // 256
WA_Q = HEADS_A * HEAD_DIM
WA_KV = KV_HEADS_A * HEAD_DIM
WB = HEADS_B * HEAD_DIM
N_IN = WA_Q + 2 * WA_KV + 3 * WB + 2 * D_MODEL
WIN_H = 8
WIN_W = 16
Q_BLOCK = 128
D_FF = ((8 * D_MODEL // 3 + 127) // 128) * 128
CONV_W = 3
ROPE_THETA = 10000.0
EPS = 1e-6

kernel_name = "hybrid_gqa_natten_convffn_block"


def rmsnorm(x, g):
    x32 = x.astype(jnp.float32)
    y = x32 * lax.rsqrt(jnp.mean(x32 * x32, axis=-1, keepdims=True) + EPS)
    return (y * g.astype(jnp.float32)).astype(x.dtype)


def rope_1d(x, pos):
    r = x.shape[-1] // 2
    freqs = ROPE_THETA ** (-jnp.arange(r, dtype=jnp.float32) / r)
    ang = pos[:, None] * freqs[None, :]
    cos = jnp.cos(ang)[None, :, None, :]
    sin = jnp.sin(ang)[None, :, None, :]
    x32 = x.astype(jnp.float32)
    x1, x2 = x32[..., :r], x32[..., r:]
    return jnp.concatenate([x1 * cos - x2 * sin, x2 * cos + x1 * sin], axis=-1).astype(x.dtype)


def rope_2d(x, row, col):
    h = x.shape[-1] // 2
    return jnp.concatenate([rope_1d(x[..., :h], row), rope_1d(x[..., h:], col)], axis=-1)


def in_proj(h, w_in):
    p = h @ w_in
    cuts = [WA_Q, WA_Q + WA_KV, WA_Q + 2 * WA_KV, WA_Q + 2 * WA_KV + WB,
            WA_Q + 2 * WA_KV + 2 * WB, WA_Q + 2 * WA_KV + 3 * WB,
            WA_Q + 2 * WA_KV + 3 * WB + D_MODEL]
    qa, ka, va, qb, kb, vb, ga, gb = jnp.split(p, cuts, axis=-1)
    B_, N_ = h.shape[:2]
    qa = qa.reshape(B_, N_, HEADS_A, HEAD_DIM)
    ka = ka.reshape(B_, N_, KV_HEADS_A, HEAD_DIM)
    va = va.reshape(B_, N_, KV_HEADS_A, HEAD_DIM)
    qb = qb.reshape(B_, N_, HEADS_B, HEAD_DIM)
    kb = kb.reshape(B_, N_, HEADS_B, HEAD_DIM)
    vb = vb.reshape(B_, N_, HEADS_B, HEAD_DIM)
    return qa, ka, va, qb, kb, vb, ga, gb


def ctx_attention(q, k, v):
    B_, L, H, hd = q.shape
    kvh = k.shape[2]
    qg = q.reshape(B_, L, kvh, H // kvh, hd)
    s = jnp.einsum('bqkgd,bskd->bkgqs', qg, k).astype(jnp.float32) * (hd ** -0.5)
    p = jax.nn.softmax(s, axis=-1).astype(v.dtype)
    o = jnp.einsum('bkgqs,bskd->bqkgd', p, v)
    return o.reshape(B_, L, H * hd)


def gqa_latent(q, k, v, k_ctx, v_ctx):
    B_, S_, H, hd = q.shape
    k_all = jnp.concatenate([k_ctx, k], axis=1)
    v_all = jnp.concatenate([v_ctx, v], axis=1)
    nblk = S_ // Q_BLOCK
    qb = q.reshape(B_, nblk, Q_BLOCK, KV_HEADS_A, GROUPS_A, hd).transpose(1, 0, 2, 3, 4, 5)

    def one_block(qblk):
        s = jnp.einsum('bqkgd,bskd->bkgqs', qblk, k_all).astype(jnp.float32) * (hd ** -0.5)
        p = jax.nn.softmax(s, axis=-1).astype(v_all.dtype)
        return jnp.einsum('bkgqs,bskd->bqkgd', p, v_all)

    o = lax.map(one_block, qb)
    return o.transpose(1, 0, 2, 3, 4, 5).reshape(B_, S_, H * hd)


def neighbourhood_latent(q, k, v, k_ctx, v_ctx, rpb):
    B_, S_, H, hd = q.shape
    rows = S_ // GRID_W
    kh = min(WIN_H, rows)
    kw = WIN_W
    qg = q.reshape(B_, rows, GRID_W, H, hd)
    kg = k.reshape(B_, rows, GRID_W, H, hd)
    vg = v.reshape(B_, rows, GRID_W, H, hd)
    col = jnp.arange(GRID_W)
    cs = jnp.clip(col - kw // 2, 0, GRID_W - kw)
    col_mask = (col[None, :] >= cs[:, None]) & (col[None, :] < cs[:, None] + kw)
    dc_idx = col[None, :] - col[:, None] + (WIN_W - 1)
    dc_idx = jnp.clip(dc_idx, 0, 2 * WIN_W - 2)
    scale = hd ** -0.5

    def one_row(args):
        r, q_row = args
        rs = jnp.clip(r - kh // 2, 0, rows - kh)
        k_rows = lax.dynamic_slice_in_dim(kg, rs, kh, axis=1)
        v_rows = lax.dynamic_slice_in_dim(vg, rs, kh, axis=1)
        dr_idx = rs + jnp.arange(kh) - r + (WIN_H - 1)
        bias = rpb[:, dr_idx][:, :, dc_idx].transpose(0, 2, 1, 3)
        s_loc = jnp.einsum('bqhd,bjkhd->bhqjk', q_row, k_rows).astype(jnp.float32) * scale
        s_loc = s_loc + bias[None].astype(jnp.float32)
        s_loc = jnp.where(col_mask[None, None, :, None, :], s_loc, -jnp.inf)
        s_loc = s_loc.reshape(B_, H, GRID_W, kh * GRID_W)
        s_ctx = jnp.einsum('bqhd,blhd->bhql', q_row, k_ctx).astype(jnp.float32) * scale
        p = jax.nn.softmax(jnp.concatenate([s_ctx, s_loc], axis=-1), axis=-1).astype(v.dtype)
        L = k_ctx.shape[1]
        o = jnp.einsum('bhql,blhd->bqhd', p[..., :L], v_ctx)
        o = o + jnp.einsum('bhqn,bnhd->bqhd', p[..., L:], v_rows.reshape(B_, kh * GRID_W, H, hd))
        return o

    o = lax.map(one_row, (jnp.arange(rows), qg.transpose(1, 0, 2, 3, 4)))
    return o.transpose(1, 0, 2, 3, 4).reshape(B_, S_, H * hd)


def merge_branches(o_a, o_b, g_a, g_b, w_br_a, w_br_b, w_out):
    y = jax.nn.sigmoid(g_a) * (o_a @ w_br_a) + jax.nn.sigmoid(g_b) * (o_b @ w_br_b)
    return y @ w_out


def conv_ffn(h, w_up, conv_w, conv_b, w_down):
    u = h @ w_up
    up = jnp.pad(u, ((0, 0), (1, 1), (0, 0)))
    u = up[:, :-2] * conv_w[0] + up[:, 1:-1] * conv_w[1] + up[:, 2:] * conv_w[2] + conv_b
    gate, val = jnp.split(u, 2, axis=-1)
    return (jax.nn.silu(gate) * val) @ w_down


def head_rms(x, g):
    return rmsnorm(x, g)


def setup_inputs(seed: int = 0) -> dict:
    key = jax.random.key(seed)
    ks = jax.random.split(key, 24)
    f32 = jnp.float32

    def nrm(k, shape, s):
        return jax.random.normal(k, shape, f32) * s

    return {
        "x": nrm(ks[0], (BATCH, SEQ, D_MODEL), 1.0),
        "c": nrm(ks[1], (BATCH, D_MODEL), 1.0),
        "ctx": nrm(ks[2], (BATCH, CTX_LEN, D_MODEL), 1.0),
        "c_ctx": nrm(ks[3], (D_MODEL,), 1.0),
        "ada_w": nrm(ks[4], (DEPTH, D_MODEL, 6 * D_MODEL), 0.5 * D_MODEL ** -0.5),
        "ada_b": nrm(ks[5], (DEPTH, 6 * D_MODEL), 0.01),
        "norm1_g": 1.0 + nrm(ks[6], (DEPTH, D_MODEL), 0.02),
        "w_in": nrm(ks[7], (DEPTH, D_MODEL, N_IN), D_MODEL ** -0.5),
        "qnorm_a": 1.0 + nrm(ks[8], (DEPTH, HEAD_DIM), 0.02),
        "knorm_a": 1.0 + nrm(ks[9], (DEPTH, HEAD_DIM), 0.02),
        "rpb_b": nrm(ks[10], (DEPTH, HEADS_B, 2 * WIN_H - 1, 2 * WIN_W - 1), 0.1),
        "w_br_a": nrm(ks[11], (DEPTH, WA_Q, D_MODEL), WA_Q ** -0.5),
        "w_br_b": nrm(ks[12], (DEPTH, WB, D_MODEL), WB ** -0.5),
        "w_out": nrm(ks[13], (DEPTH, D_MODEL, D_MODEL), D_MODEL ** -0.5),
        "norm2_g": 1.0 + nrm(ks[14], (DEPTH, D_MODEL), 0.02),
        "w_up": nrm(ks[15], (DEPTH, D_MODEL, 2 * D_FF), D_MODEL ** -0.5),
        "conv_w": nrm(ks[16], (DEPTH, CONV_W, 2 * D_FF), CONV_W ** -0.5),
        "conv_b": nrm(ks[17], (DEPTH, 2 * D_FF), 0.01),
        "w_down": nrm(ks[18], (DEPTH, D_FF, D_MODEL), D_FF ** -0.5),
        "final_g": 1.0 + nrm(ks[19], (D_MODEL,), 0.02),
    }


def reference(x, c, ctx, c_ctx, ada_w, ada_b, norm1_g, w_in, qnorm_a, knorm_a, rpb_b,
              w_br_a, w_br_b, w_out, norm2_g, w_up, conv_w, conv_b, w_down, final_g):
    B_, S_, _ = x.shape
    t = jnp.arange(S_)
    row = (t // GRID_W).astype(jnp.float32)
    col = (t % GRID_W).astype(jnp.float32)

    for layer in range(DEPTH):
        mod = jax.nn.silu(c) @ ada_w[layer] + ada_b[layer]
        sh1, sc1, g1, sh2, sc2, g2 = jnp.split(mod[:, None, :], 6, axis=-1)
        mod_c = jax.nn.silu(c_ctx) @ ada_w[layer] + ada_b[layer]
        csh1, csc1, cg1, csh2, csc2, cg2 = jnp.split(mod_c, 6, axis=-1)

        hc = rmsnorm(ctx, norm1_g[layer]) * (1 + csc1) + csh1
        qa_c, ka_c, va_c, qb_c, kb_c, vb_c, ga_c, gb_c = in_proj(hc, w_in[layer])
        ka_c = head_rms(ka_c, knorm_a[layer])

        h = rmsnorm(x, norm1_g[layer]) * (1 + sc1) + sh1
        qa, ka, va, qb, kb, vb, ga, gb = in_proj(h, w_in[layer])
        qa = rope_2d(head_rms(qa, qnorm_a[layer]), row, col)
        ka = rope_2d(head_rms(ka, knorm_a[layer]), row, col)
        o_a = gqa_latent(qa, ka, va, ka_c, va_c)
        o_b = neighbourhood_latent(qb, kb, vb, kb_c, vb_c, rpb_b[layer])
        x = x + g1 * merge_branches(o_a, o_b, ga, gb, w_br_a[layer], w_br_b[layer], w_out[layer])
        h2 = rmsnorm(x, norm2_g[layer]) * (1 + sc2) + sh2
        x = x + g2 * conv_ffn(h2, w_up[layer], conv_w[layer], conv_b[layer], w_down[layer])

        if layer < DEPTH - 1:
            qa_c = head_rms(qa_c, qnorm_a[layer])
            oa_c = ctx_attention(qa_c, ka_c, va_c)
            ob_c = ctx_attention(qb_c, kb_c, vb_c)
            ctx = ctx + cg1 * merge_branches(oa_c, ob_c, ga_c, gb_c, w_br_a[layer], w_br_b[layer], w_out[layer])
            hc2 = rmsnorm(ctx, norm2_g[layer]) * (1 + csc2) + csh2
            ctx = ctx + cg2 * conv_ffn(hc2, w_up[layer], conv_w[layer], conv_b[layer], w_down[layer])

    return rmsnorm(x, final_g)
```

```python
import functools

import jax
import jax.numpy as jnp
from jax import lax
from jax.experimental import pallas as pl
from jax.experimental.pallas import tpu as pltpu

_HEAD_DIM = 128
_GRID_W = 64
_WIN_H = 8
_WIN_W = 16
_GROUPS_A = 4
_ROPE_THETA = 10000.0
_EPS = 1e-6
_BLK_ROWS = 8
_KEY_ROWS = 2 * _BLK_ROWS
_V7X_VMEM_BYTES = 64 * 1024 * 1024
_VMEM_CAP = _V7X_VMEM_BYTES - 6 * 1024 * 1024
_NEG_INF = float("-inf")

_BF16 = jnp.bfloat16
_F32 = jnp.float32


def _params(vmem_bytes, semantics):
    limit = int(min(max(vmem_bytes, 32 * 1024 * 1024), _VMEM_CAP))
    return pltpu.CompilerParams(dimension_semantics=semantics, vmem_limit_bytes=limit)


def _dot(a, b):
    return jnp.dot(a, b, preferred_element_type=_F32)


def _dot_nt(a, b):
    return lax.dot_general(a, b, (((1,), (1,)), ((), ())), preferred_element_type=_F32)


def _pick_tile(n, candidates):
    for t in candidates:
        if n % t == 0:
            return t
    raise ValueError(f"no tile in {candidates} divides {n}")


def _mod_kernel(c_ref, w_ref, b_ref, o_ref):
    c = c_ref[...]
    s = (c * jax.nn.sigmoid(c)).astype(_BF16)
    o_ref[...] = _dot(s, w_ref[...].astype(_BF16)) + b_ref[...]


def _modulation(cc, ada_w, ada_b):
    rows, d = cc.shape
    n = ada_w.shape[1]
    tn = _pick_tile(n, (512, 256, 128))
    vmem = 2 * (d * tn * 4) + d * tn * 2 + 4 * rows * d * 4 + (4 << 20)
    return pl.pallas_call(
        _mod_kernel,
        grid=(n // tn,),
        in_specs=[
            pl.BlockSpec((rows, d), lambda j: (0, 0)),
            pl.BlockSpec((d, tn), lambda j: (0, j)),
            pl.BlockSpec((1, tn), lambda j: (0, j)),
        ],
        out_specs=pl.BlockSpec((rows, tn), lambda j: (0, j)),
        out_shape=jax.ShapeDtypeStruct((rows, n), _F32),
        compiler_params=_params(vmem, ("arbitrary",)),
        name="adaln_mod",
    )(cc, ada_w, ada_b)


def _norm_kernel(modulate, x_ref, g_ref, *rest):
    if modulate:
        sc_ref, sh_ref, o_ref = rest
    else:
        (o_ref,) = rest
    x = x_ref[...]
    ms = jnp.mean(x * x, axis=-1, keepdims=True)
    y = x * lax.rsqrt(ms + _EPS) * g_ref[...]
    if modulate:
        y = y * (1.0 + sc_ref[0]) + sh_ref[0]
    o_ref[...] = y.astype(o_ref.dtype)


def _rmsnorm(x, g, sc, sh, rows_per_group, out_dtype):
    m, d = x.shape
    tm = _pick_tile(rows_per_group, (512, 256, 128, 64, 32, 16, 8))
    tiles_per_group = rows_per_group // tm
    modulate = sc is not None
    in_specs = [pl.BlockSpec((tm, d), lambda i: (i, 0)), pl.BlockSpec((1, d), lambda i: (0, 0))]
    args = [x, g]
    if modulate:
        mod_spec = pl.BlockSpec((1, 1, d), lambda i: (i // tiles_per_group, 0, 0))
        in_specs += [mod_spec, mod_spec]
        args += [sc, sh]
    vmem = 2 * tm * d * 4 + 2 * tm * d * jnp.dtype(out_dtype).itemsize + 3 * tm * d * 4 + (4 << 20)
    return pl.pallas_call(
        functools.partial(_norm_kernel, modulate),
        grid=(m // tm,),
        in_specs=in_specs,
        out_specs=pl.BlockSpec((tm, d), lambda i: (i, 0)),
        out_shape=jax.ShapeDtypeStruct((m, d), out_dtype),
        compiler_params=_params(vmem, ("arbitrary",)),
        name="rmsnorm_mod" if modulate else "rmsnorm",
    )(*args)


def _rope_table_kernel(freq_ref, cos_ref, sin_ref):
    shape = cos_ref.shape
    t = lax.broadcasted_iota(jnp.int32, shape, 0)
    lane = lax.broadcasted_iota(jnp.int32, shape, 1)
    row = lax.shift_right_logical(t, _GRID_W.bit_length() - 1)
    col = jnp.bitwise_and(t, _GRID_W - 1)
    pos = jnp.where(lane < _HEAD_DIM // 2, row, col).astype(_F32)
    ang = pos * freq_ref[...]
    first_half = jnp.bitwise_and(lane, _HEAD_DIM // 2 - 1) < _HEAD_DIM // 4
    cos_ref[...] = jnp.cos(ang)
    sin_ref[...] = jnp.where(first_half, -jnp.sin(ang), jnp.sin(ang))


def _rope_tables(seq):
    r = _HEAD_DIM // 4
    freqs = _ROPE_THETA ** (-jnp.arange(r, dtype=_F32) / r)
    freq_lanes = jnp.tile(freqs, _HEAD_DIM // r)[None, :]
    out = jax.ShapeDtypeStruct((seq, _HEAD_DIM), _F32)
    return pl.pallas_call(
        _rope_table_kernel,
        out_shape=(out, out),
        name="rope_tables",
    )(freq_lanes)


def _inproj_kernel(kind, cm, h_ref, w_ref, *rest):
    if kind == "normrope":
        gain_ref, cos_ref, sin_ref, o_ref = rest
    else:
        (o_ref,) = rest
    rows, tn = o_ref.shape
    w = w_ref[...]
    if kind == "normrope":
        gain = gain_ref[0]
        lane = lax.broadcasted_iota(jnp.int32, (cm, _HEAD_DIM), 1)
        first_half = jnp.bitwise_and(lane, _HEAD_DIM // 2 - 1) < _HEAD_DIM // 4
    for c in range(rows // cm):
        rs = slice(c * cm, (c + 1) * cm)
        acc = _dot(h_ref[rs, :], w)
        if kind == "plain":
            o_ref[rs, :] = acc.astype(o_ref.dtype)
        elif kind == "sigmoid":
            o_ref[rs, :] = jax.nn.sigmoid(acc).astype(o_ref.dtype)
        else:
            cos = cos_ref[rs, :]
            sin = sin_ref[rs, :]
            for hc in range(tn // _HEAD_DIM):
                cs = slice(hc * _HEAD_DIM, (hc + 1) * _HEAD_DIM)
                xh = acc[:, cs]
                ms = jnp.mean(xh * xh, axis=-1, keepdims=True)
                y = xh * lax.rsqrt(ms + _EPS) * gain
                partner = jnp.where(
                    first_half,
                    pltpu.roll(y, _HEAD_DIM - _HEAD_DIM // 4, axis=1),
                    pltpu.roll(y, _HEAD_DIM // 4, axis=1),
                )
                o_ref[rs, cs] = (y * cos + partner * sin).astype(o_ref.dtype)


def _inproj(h, w, kind, slab_rows, gains=None, cos=None, sin=None):
    m, d = h.shape
    n = w.shape[1]
    tn = _pick_tile(n, (512, 256, 128))
    cm = _pick_tile(slab_rows, (512, 256, 128, 64, 32, 16))
    in_specs = [
        pl.BlockSpec((slab_rows, d), lambda b, j: (b, 0), pipeline_mode=pl.Buffered(1)),
        pl.BlockSpec((d, tn), lambda b, j: (0, j)),
    ]
    args = [h, w]
    vmem = slab_rows * d * 2 + 2 * d * tn * 2 + 2 * slab_rows * tn * 2 + 6 * cm * tn * 4 + (6 << 20)
    if kind == "normrope":
        in_specs += [
            pl.BlockSpec((1, 1, _HEAD_DIM), lambda b, j: (j, 0, 0)),
            pl.BlockSpec((slab_rows, _HEAD_DIM), lambda b, j: (0, 0), pipeline_mode=pl.Buffered(1)),
            pl.BlockSpec((slab_rows, _HEAD_DIM), lambda b, j: (0, 0), pipeline_mode=pl.Buffered(1)),
        ]
        args += [gains, cos, sin]
        vmem += 2 * slab_rows * _HEAD_DIM * 4
    return pl.pallas_call(
        functools.partial(_inproj_kernel, kind, cm),
        grid=(m // slab_rows, n // tn),
        in_specs=in_specs,
        out_specs=pl.BlockSpec((slab_rows, tn), lambda b, j: (b, j)),
        out_shape=jax.ShapeDtypeStruct((m, n), _BF16),
        compiler_params=_params(vmem, ("arbitrary", "arbitrary")),
        name="inproj_" + kind,
    )(*args)


def _attn_a_kernel(q_ref, k_ref, v_ref, kc_ref, vc_ref, o_ref):
    k = k_ref[...]
    v = v_ref[...]
    kc = kc_ref[...]
    vc = vc_ref[...]
    for g in range(_GROUPS_A):
        cs = slice(g * _HEAD_DIM, (g + 1) * _HEAD_DIM)
        q = q_ref[:, cs]
        s_c = _dot_nt(q, kc)
        s_l = _dot_nt(q, k)
        m = jnp.maximum(jnp.max(s_c, axis=-1, keepdims=True), jnp.max(s_l, axis=-1, keepdims=True))
        p_c = jnp.exp(s_c - m)
        p_l = jnp.exp(s_l - m)
        denom = jnp.sum(p_c, axis=-1, keepdims=True) + jnp.sum(p_l, axis=-1, keepdims=True)
        o = _dot(p_c.astype(_BF16), vc) + _dot(p_l.astype(_BF16), v)
        o_ref[:, cs] = (o / denom).astype(o_ref.dtype)


def _attn_a(qk, pv, ka_c, pv_c, batch, seq, ctx_len, kv_heads):
    wq = kv_heads * _GROUPS_A * _HEAD_DIM
    gw = _GROUPS_A * _HEAD_DIM
    tq = _pick_tile(seq, (256, 128))
    nq = seq // tq
    koff = wq // _HEAD_DIM
    vmem = (2 * tq * gw * 2 * 2 + 4 * seq * _HEAD_DIM * 2 + 4 * ctx_len * _HEAD_DIM * 2
            + 4 * tq * (seq + ctx_len) * 4 + (6 << 20))
    return pl.pallas_call(
        _attn_a_kernel,
        grid=(batch, kv_heads, nq),
        in_specs=[
            pl.BlockSpec((tq, gw), lambda b, h, i: (b * nq + i, h)),
            pl.BlockSpec((seq, _HEAD_DIM), lambda b, h, i: (b, koff + h)),
            pl.BlockSpec((seq, _HEAD_DIM), lambda b, h, i: (b, h)),
            pl.BlockSpec((ctx_len, _HEAD_DIM), lambda b, h, i: (b, h)),
            pl.BlockSpec((ctx_len, _HEAD_DIM), lambda b, h, i: (b, h)),
        ],
        out_specs=pl.BlockSpec((tq, gw), lambda b, h, i: (b * nq + i, h)),
        out_shape=jax.ShapeDtypeStruct((batch * seq, wq), _BF16),
        compiler_params=_params(vmem, ("arbitrary", "arbitrary", "arbitrary")),
        name="attn_gqa",
    )(qk, qk, pv, ka_c, pv_c)


def _window_start_row(blk, grid_rows):
    lo = _BLK_ROWS * blk - _WIN_H // 2
    if isinstance(blk, int):
        return min(max(lo, 0), grid_rows - _KEY_ROWS)
    return jnp.clip(lo, 0, grid_rows - _KEY_ROWS)


def _bias_table_kernel(grid_rows, rpb_ref, o_ref):
    h = pl.program_id(0)
    n_dr = 2 * _WIN_H - 1
    n_dc = 2 * _WIN_W - 1
    shape = (_GRID_W, 2 * _GRID_W)
    qc = lax.broadcasted_iota(jnp.int32, shape, 0)
    lane = lax.broadcasted_iota(jnp.int32, shape, 1)
    col_start = jnp.clip(qc - _WIN_W // 2, 0, _GRID_W - _WIN_W)
    neg = jnp.full(shape, _NEG_INF, _F32)

    def side(kc, on_side):
        ok = on_side & (kc >= col_start) & (kc < col_start + _WIN_W)
        return kc - qc + (_WIN_W - 1), ok

    diff_l, ok_l = side(lane, lane < _GRID_W)
    diff_r, ok_r = side(lane - _GRID_W, lane >= _GRID_W)
    left, right = [], []
    for dr in range(n_dr):
        acc_l, acc_r = neg, neg
        for dc in range(n_dc):
            val = rpb_ref[(h * n_dr + dr) * n_dc + dc]
            acc_l = jnp.where(ok_l & (diff_l == dc), val, acc_l)
            acc_r = jnp.where(ok_r & (diff_r == dc), val, acc_r)
        left.append(acc_l)
        right.append(acc_r)

    n_blk = grid_rows // _BLK_ROWS
    for variant, blk in enumerate((0, 1, n_blk - 1)):
        ws = _window_start_row(blk, grid_rows)
        for a in range(_BLK_ROWS):
            qr = _BLK_ROWS * blk + a
            rs = min(max(qr - _WIN_H // 2, 0), grid_rows - _WIN_H)
            for jp in range(_KEY_ROWS // 2):
                piece = None
                for half, table in ((0, left), (1, right)):
                    kr = ws + 2 * jp + half
                    if rs <= kr < rs + _WIN_H:
                        blk_bias = table[kr - qr + _WIN_H - 1]
                        piece = blk_bias if piece is None else jnp.maximum(piece, blk_bias)
                if piece is None:
                    piece = neg
                o_ref[0, variant, a * _GRID_W:(a + 1) * _GRID_W,
                      jp * 2 * _GRID_W:(jp + 1) * 2 * _GRID_W] = piece


def _bias_tables(rpb, grid_rows):
    heads = rpb.shape[0]
    blk = (1, 3, _BLK_ROWS * _GRID_W, _KEY_ROWS * _GRID_W)
    return pl.pallas_call(
        functools.partial(_bias_table_kernel, grid_rows),
        grid=(heads,),
        in_specs=[pl.BlockSpec(memory_space=pltpu.SMEM)],
        out_specs=pl.BlockSpec(blk, lambda h: (h, 0, 0, 0)),
        out_shape=jax.ShapeDtypeStruct((heads,) + blk[1:], _F32),
        compiler_params=_params(3 * blk[1] * blk[2] * blk[3] * 4 + (8 << 20), ("arbitrary",)),
        name="nbr_bias_tables",
    )(rpb.reshape(-1))


def _attn_b_kernel(grid_rows, q_ref, k_ref, v_ref, kc_ref, vc_ref, bias_ref, o_ref):
    blk = pl.program_id(2)
    start = pl.multiple_of(_window_start_row(blk, grid_rows) * _GRID_W, _GRID_W)
    span = _KEY_ROWS * _GRID_W
    kw = k_ref[pl.ds(start, span), :]
    vw = v_ref[pl.ds(start, span), :]
    q = q_ref[...]
    scale = _HEAD_DIM ** -0.5
    s_l = _dot_nt(q, kw) * scale + bias_ref[0, 0]
    s_c = _dot_nt(q, kc_ref[...]) * scale
    m = jnp.maximum(jnp.max(s_c, axis=-1, keepdims=True), jnp.max(s_l, axis=-1, keepdims=True))
    p_c = jnp.exp(s_c - m)
    p_l = jnp.exp(s_l - m)
    denom = jnp.sum(p_c, axis=-1, keepdims=True) + jnp.sum(p_l, axis=-1, keepdims=True)
    o = _dot(p_c.astype(_BF16), vc_ref[...]) + _dot(p_l.astype(_BF16), vw)
    o_ref[...] = (o / denom).astype(o_ref.dtype)


def _attn_b(pv, pv_c, bias, batch, seq, ctx_len, heads, q_off, k_off, v_off, kc_off, vc_off):
    grid_rows = seq // _GRID_W
    n_blk = grid_rows // _BLK_ROWS
    tq = _BLK_ROWS * _GRID_W
    span = _KEY_ROWS * _GRID_W

    def variant(i):
        return jnp.where(i == 0, 0, jnp.where(i == n_blk - 1, 2, 1))

    vmem = (4 * tq * _HEAD_DIM * 2 + 4 * seq * _HEAD_DIM * 2 + 4 * ctx_len * _HEAD_DIM * 2
            + 2 * tq * span * 4 + 5 * tq * (span + ctx_len) * 4 + (6 << 20))
    return pl.pallas_call(
        functools.partial(_attn_b_kernel, grid_rows),
        grid=(batch, heads, n_blk),
        in_specs=[
            pl.BlockSpec((tq, _HEAD_DIM), lambda b, h, i: (b * n_blk + i, q_off + h)),
            pl.BlockSpec((seq, _HEAD_DIM), lambda b, h, i: (b, k_off + h)),
            pl.BlockSpec((seq, _HEAD_DIM), lambda b, h, i: (b, v_off + h)),
            pl.BlockSpec((ctx_len, _HEAD_DIM), lambda b, h, i: (b, kc_off + h)),
            pl.BlockSpec((ctx_len, _HEAD_DIM), lambda b, h, i: (b, vc_off + h)),
            pl.BlockSpec((1, 1, tq, span), lambda b, h, i: (h, variant(i), 0, 0)),
        ],
        out_specs=pl.BlockSpec((tq, _HEAD_DIM), lambda b, h, i: (b * n_blk + i, h)),
        out_shape=jax.ShapeDtypeStruct((batch * seq, heads * _HEAD_DIM), _BF16),
        compiler_params=_params(vmem, ("arbitrary", "arbitrary", "arbitrary")),
        name="attn_nbr",
    )(pv, pv, pv, pv_c, pv_c, bias)


def _merge_kernel(oa_ref, ob_ref, wa_ref, wb_ref, ga_ref, gb_ref, y_ref):
    ya = _dot(oa_ref[...], wa_ref[...])
    yb = _dot(ob_ref[...], wb_ref[...])
    y = ga_ref[...].astype(_F32) * ya + gb_ref[...].astype(_F32) * yb
    y_ref[...] = y.astype(y_ref.dtype)


def _merge(o_a, o_b, w_a, w_b, gates, d_model):
    m, ka = o_a.shape
    kb = o_b.shape[1]
    tm = _pick_tile(m, (1024, 512, 256, 128))
    tn = _pick_tile(d_model, (1024, 512, 256, 128))
    nj = d_model // tn
    vmem = 2 * (tm * (ka + kb) * 2 + (ka + kb) * tn * 2 + 3 * tm * tn * 2) + 4 * tm * tn * 4 + (6 << 20)
    return pl.pallas_call(
        _merge_kernel,
        grid=(m // tm, nj),
        in_specs=[
            pl.BlockSpec((tm, ka), lambda i, j: (i, 0)),
            pl.BlockSpec((tm, kb), lambda i, j: (i, 0)),
            pl.BlockSpec((ka, tn), lambda i, j: (0, j)),
            pl.BlockSpec((kb, tn), lambda i, j: (0, j)),
            pl.BlockSpec((tm, tn), lambda i, j: (i, j)),
            pl.BlockSpec((tm, tn), lambda i, j: (i, nj + j)),
        ],
        out_specs=pl.BlockSpec((tm, tn), lambda i, j: (i, j)),
        out_shape=jax.ShapeDtypeStruct((m, d_model), _BF16),
        compiler_params=_params(vmem, ("arbitrary", "arbitrary")),
        name="branch_merge",
    )(o_a, o_b, w_a, w_b, gates, gates)


def _mm_residual_kernel(a_ref, w_ref, res_ref, gate_ref, o_ref):
    acc = _dot(a_ref[...], w_ref[...])
    o_ref[...] = res_ref[...] + gate_ref[0] * acc


def _mm_residual(a, w, res, gate, rows_per_group, tm_cands, tn_cands, name):
    m, k = a.shape
    n = w.shape[1]
    tm = _pick_tile(rows_per_group, tm_cands)
    tn = _pick_tile(n, tn_cands)
    tiles_per_group = rows_per_group // tm
    vmem = 2 * (tm * k * 2 + k * tn * 2 + 2 * tm * tn * 4) + 2 * tm * tn * 4 + (6 << 20)
    return pl.pallas_call(
        _mm_residual_kernel,
        grid=(m // tm, n // tn),
        in_specs=[
            pl.BlockSpec((tm, k), lambda i, j: (i, 0)),
            pl.BlockSpec((k, tn), lambda i, j: (0, j)),
            pl.BlockSpec((tm, tn), lambda i, j: (i, j)),
            pl.BlockSpec((1, 1, tn), lambda i, j: (i // tiles_per_group, 0, j)),
        ],
        out_specs=pl.BlockSpec((tm, tn), lambda i, j: (i, j)),
        out_shape=jax.ShapeDtypeStruct((m, n), _F32),
        compiler_params=_params(vmem, ("arbitrary", "arbitrary")),
        name=name,
    )(a, w, res, gate)


def _ffn_up_kernel(cm, h_ref, wg_ref, wv_ref, cwg_ref, cwv_ref, cbg_ref, cbv_ref, o_ref):
    rows, tn = o_ref.shape
    n_chunks = rows // cm
    wg = wg_ref[...]
    wv = wv_ref[...]
    ug, uv = [], []
    for c in range(n_chunks):
        hs = h_ref[c * cm:(c + 1) * cm, :]
        ug.append(_dot(hs, wg))
        uv.append(_dot(hs, wv))
    row = lax.broadcasted_iota(jnp.int32, (cm, tn), 0)
    zero_row = jnp.zeros((1, tn), _F32)

    def conv(u, c, cw_ref, cb_ref):
        before = u[c - 1][cm - 1:cm, :] if c > 0 else zero_row
        after = u[c + 1][0:1, :] if c + 1 < n_chunks else zero_row
        prev = jnp.where(row == 0, before, pltpu.roll(u[c], 1, axis=0))
        nxt = jnp.where(row == cm - 1, after, pltpu.roll(u[c], cm - 1, axis=0))
        return prev * cw_ref[0:1, :] + u[c] * cw_ref[1:2, :] + nxt * cw_ref[2:3, :] + cb_ref[...]

    for c in range(n_chunks):
        gate = conv(ug, c, cwg_ref, cbg_ref)
        val = conv(uv, c, cwv_ref, cbv_ref)
        o_ref[c * cm:(c + 1) * cm, :] = (gate * jax.nn.sigmoid(gate) * val).astype(o_ref.dtype)


def _ffn_up(h2, w_up, conv_w, conv_b, seq):
    m, d = h2.shape
    d_ff = w_up.shape[1] // 2
    tn = _pick_tile(d_ff, (256, 128))
    nj = d_ff // tn
    cm = _pick_tile(seq, (512, 256, 128, 64, 32, 16))
    taps = conv_w.shape[0]
    vmem = seq * d * 2 + 2 * 2 * d * tn * 2 + 2 * seq * tn * 2 + 6 * seq * tn * 4 + (6 << 20)
    return pl.pallas_call(
        functools.partial(_ffn_up_kernel, cm),
        grid=(m // seq, nj),
        in_specs=[
            pl.BlockSpec((seq, d), lambda b, j: (b, 0), pipeline_mode=pl.Buffered(1)),
            pl.BlockSpec((d, tn), lambda b, j: (0, j)),
            pl.BlockSpec((d, tn), lambda b, j: (0, nj + j)),
            pl.BlockSpec((taps, tn), lambda b, j: (0, j)),
            pl.BlockSpec((taps, tn), lambda b, j: (0, nj + j)),
            pl.BlockSpec((1, tn), lambda b, j: (0, j)),
            pl.BlockSpec((1, tn), lambda b, j: (0, nj + j)),
        ],
        out_specs=pl.BlockSpec((seq, tn), lambda b, j: (b, j)),
        out_shape=jax.ShapeDtypeStruct((m, d_ff), _BF16),
        compiler_params=_params(vmem, ("arbitrary", "arbitrary")),
        name="ffn_up_conv_gate",
    )(h2, w_up, w_up, conv_w, conv_w, conv_b, conv_b)


def kernel(x, c, ctx, c_ctx, ada_w, ada_b, norm1_g, w_in, qnorm_a, knorm_a, rpb_b,
           w_br_a, w_br_b, w_out, norm2_g, w_up, conv_w, conv_b, w_down, final_g):
    batch, seq, d = x.shape
    ctx_len = ctx.shape[1]
    depth = ada_w.shape[0]
    wa_q = w_br_a.shape[1]
    wb = w_br_b.shape[1]
    wa_kv = (w_in.shape[2] - wa_q - 3 * wb - 2 * d) // 2
    kv_heads = wa_kv // _HEAD_DIM
    heads_b = wb // _HEAD_DIM
    assert depth == 1, "the context-stream update between layers is not implemented"
    assert wa_q == kv_heads * _GROUPS_A * _HEAD_DIM and seq % (_BLK_ROWS * _GRID_W) == 0
    assert seq // _GRID_W >= _KEY_ROWS and qnorm_a.shape[-1] == _HEAD_DIM
    grid_rows = seq // _GRID_W
    m = batch * seq
    xf = x.reshape(m, d)
    ctxf = ctx.reshape(batch * ctx_len, d)

    pad_rows = -(batch + 1) % 16
    cc = jnp.concatenate([c, c_ctx[None, :], jnp.zeros((pad_rows, d), _F32)], axis=0)
    mod = _modulation(cc, ada_w[0], ada_b[0][None, :])
    sh1, sc1, g1, sh2, sc2, g2 = [mod[:batch, i * d:(i + 1) * d][:, None, :] for i in range(6)]
    csh1, csc1 = [mod[batch:batch + 1, i * d:(i + 1) * d][:, None, :] for i in range(2)]

    w_in16 = w_in[0]
    o_k = wa_q
    o_v = o_k + wa_kv
    o_qb = o_v + wa_kv
    o_kb = o_qb + wb
    o_g = o_qb + 3 * wb
    w_qk = w_in16[:, :o_v].astype(_BF16)
    w_pv = w_in16[:, o_v:o_g].astype(_BF16)
    w_gate = w_in16[:, o_g:].astype(_BF16)
    w_kc = w_in16[:, o_k:o_v].astype(_BF16)
    w_pvc = jnp.concatenate([w_in16[:, o_v:o_qb], w_in16[:, o_kb:o_g]], axis=1).astype(_BF16)

    g1n = norm1_g[0][None, :]
    h = _rmsnorm(xf, g1n, sc1, sh1, seq, _BF16)
    hc = _rmsnorm(ctxf, g1n, csc1, csh1, batch * ctx_len, _BF16)

    cos_t, sin_t = _rope_tables(seq)
    scale = _HEAD_DIM ** -0.5
    tn_qk = _pick_tile(o_v, (512, 256, 128))
    q_tiles = wa_q // tn_qk
    assert wa_q % tn_qk == 0
    gain_q = jnp.broadcast_to(qnorm_a[0] * scale, (q_tiles, 1, _HEAD_DIM))
    gain_k = jnp.broadcast_to(knorm_a[0], (wa_kv // tn_qk, 1, _HEAD_DIM))
    qk = _inproj(h, w_qk, "normrope", seq, jnp.concatenate([gain_q, gain_k], axis=0), cos_t, sin_t)
    pv = _inproj(h, w_pv, "plain", seq)
    gates = _inproj(h, w_gate, "sigmoid", seq)
    c_rows = batch * ctx_len
    tn_kc = _pick_tile(wa_kv, (512, 256, 128))
    gain_kc = jnp.broadcast_to(knorm_a[0], (wa_kv // tn_kc, 1, _HEAD_DIM))
    ka_c = _inproj(hc, w_kc, "normrope", c_rows, gain_kc,
                   jnp.ones((c_rows, _HEAD_DIM), _F32), jnp.zeros((c_rows, _HEAD_DIM), _F32))
    pv_c = _inproj(hc, w_pvc, "plain", c_rows)

    o_a = _attn_a(qk, pv, ka_c, pv_c, batch, seq, ctx_len, kv_heads)
    bias = _bias_tables(rpb_b[0], grid_rows)
    o_b = _attn_b(pv, pv_c, bias, batch, seq, ctx_len, heads_b,
                  q_off=kv_heads, k_off=kv_heads + heads_b, v_off=kv_heads + 2 * heads_b,
                  kc_off=kv_heads, vc_off=kv_heads + heads_b)

    y = _merge(o_a, o_b, w_br_a[0].astype(_BF16), w_br_b[0].astype(_BF16), gates, d)
    x1 = _mm_residual(y, w_out[0].astype(_BF16), xf, g1, seq,
                      (1024, 512, 256, 128), (1024, 512, 256, 128), "out_proj_residual")

    h2 = _rmsnorm(x1, norm2_g[0][None, :], sc2, sh2, seq, _BF16)
    t = _ffn_up(h2, w_up[0].astype(_BF16), conv_w[0], conv_b[0][None, :], seq)
    x2 = _mm_residual(t, w_down[0].astype(_BF16), x1, g2, seq,
                      (512, 256, 128), (512, 256, 128), "ffn_down_residual")

    out = _rmsnorm(x2, final_g[None, :], None, None, seq, _F32)
    return out.reshape(batch, seq, d)
```

```python
import functools
import math

import jax
import jax.numpy as jnp
from jax import lax
from jax.experimental import pallas as pl
from jax.experimental.pallas import tpu as pltpu

_HEAD_DIM = 128
_GRID_W = 64
_WIN_H = 8
_WIN_W = 16
_GROUPS_A = 4
_ROPE_THETA = 10000.0
_EPS = 1e-6
_BLK_ROWS = 8
_KEY_ROWS = 2 * _BLK_ROWS
_SUBLANES = 8
_V7X_VMEM_BYTES = 64 * 1024 * 1024
_VMEM_CAP = _V7X_VMEM_BYTES - 6 * 1024 * 1024
_NEG_INF = float("-inf")
_LOG2_E = 1.4426950408889634
_QK_SCALE = _LOG2_E * _HEAD_DIM ** -0.5

_BF16 = jnp.bfloat16
_F32 = jnp.float32


def _params(vmem_bytes, semantics):
    limit = int(min(max(vmem_bytes, 32 * 1024 * 1024), _VMEM_CAP))
    return pltpu.CompilerParams(dimension_semantics=semantics, vmem_limit_bytes=limit)


def _dot(a, b):
    return jnp.dot(a, b, preferred_element_type=_F32)


def _dot_nt(a, b):
    return lax.dot_general(a, b, (((1,), (1,)), ((), ())), preferred_element_type=_F32)


def _pick_tile(n, candidates):
    for t in candidates:
        if n % t == 0:
            return t
    raise ValueError(f"no tile in {candidates} divides {n}")


def _mod_kernel(c_ref, w_ref, b_ref, o_ref):
    c = c_ref[...]
    s = (c * jax.nn.sigmoid(c)).astype(_BF16)
    o_ref[...] = _dot(s, w_ref[...].astype(_BF16)) + b_ref[...]


def _modulation(cc, ada_w, ada_b):
    rows, d = cc.shape
    n = ada_w.shape[1]
    tn = _pick_tile(n, (512, 256, 128))
    vmem = 2 * (d * tn * 4) + d * tn * 2 + 4 * rows * d * 4 + (4 << 20)
    return pl.pallas_call(
        _mod_kernel,
        grid=(n // tn,),
        in_specs=[
            pl.BlockSpec((rows, d), lambda j: (0, 0)),
            pl.BlockSpec((d, tn), lambda j: (0, j)),
            pl.BlockSpec((1, tn), lambda j: (0, j)),
        ],
        out_specs=pl.BlockSpec((rows, tn), lambda j: (0, j)),
        out_shape=jax.ShapeDtypeStruct((rows, n), _F32),
        compiler_params=_params(vmem, ("arbitrary",)),
        name="adaln_mod",
    )(cc, ada_w, ada_b)


def _norm_kernel(modulate, x_ref, g_ref, *rest):
    if modulate:
        sc_ref, sh_ref, o_ref = rest
    else:
        (o_ref,) = rest
    x = x_ref[...]
    ms = jnp.mean(x * x, axis=-1, keepdims=True)
    y = x * lax.rsqrt(ms + _EPS) * g_ref[...]
    if modulate:
        y = y * (1.0 + sc_ref[0]) + sh_ref[0]
    o_ref[...] = y.astype(o_ref.dtype)


def _rmsnorm(x, g, sc, sh, rows_per_group, out_dtype):
    m, d = x.shape
    tm = _pick_tile(rows_per_group, (512, 256, 128, 64, 32, 16, 8))
    tiles_per_group = rows_per_group // tm
    modulate = sc is not None
    in_specs = [pl.BlockSpec((tm, d), lambda i: (i, 0)), pl.BlockSpec((1, d), lambda i: (0, 0))]
    args = [x, g]
    if modulate:
        mod_spec = pl.BlockSpec((1, 1, d), lambda i: (i // tiles_per_group, 0, 0))
        in_specs += [mod_spec, mod_spec]
        args += [sc, sh]
    vmem = 2 * tm * d * 4 + 2 * tm * d * jnp.dtype(out_dtype).itemsize + 3 * tm * d * 4 + (4 << 20)
    return pl.pallas_call(
        functools.partial(_norm_kernel, modulate),
        grid=(m // tm,),
        in_specs=in_specs,
        out_specs=pl.BlockSpec((tm, d), lambda i: (i, 0)),
        out_shape=jax.ShapeDtypeStruct((m, d), out_dtype),
        compiler_params=_params(vmem, ("arbitrary",)),
        name="rmsnorm_mod" if modulate else "rmsnorm",
    )(*args)


def _rope_table_kernel(freq_ref, cos_ref, sin_ref):
    shape = cos_ref.shape
    t = lax.broadcasted_iota(jnp.int32, shape, 0)
    lane = lax.broadcasted_iota(jnp.int32, shape, 1)
    row = lax.shift_right_logical(t, _GRID_W.bit_length() - 1)
    col = jnp.bitwise_and(t, _GRID_W - 1)
    pos = jnp.where(lane < _HEAD_DIM // 2, row, col).astype(_F32)
    ang = pos * freq_ref[...]
    first_half = jnp.bitwise_and(lane, _HEAD_DIM // 2 - 1) < _HEAD_DIM // 4
    cos_ref[...] = jnp.cos(ang)
    sin_ref[...] = jnp.where(first_half, -jnp.sin(ang), jnp.sin(ang))


def _rope_tables(seq):
    r = _HEAD_DIM // 4
    freqs = _ROPE_THETA ** (-jnp.arange(r, dtype=_F32) / r)
    freq_lanes = jnp.tile(freqs, _HEAD_DIM // r)[None, :]
    out = jax.ShapeDtypeStruct((seq, _HEAD_DIM), _F32)
    return pl.pallas_call(
        _rope_table_kernel,
        out_shape=(out, out),
        name="rope_tables",
    )(freq_lanes)


def _inproj_kernel(kind, cm, h_ref, w_ref, *rest):
    if kind == "normrope":
        gain_ref, cos_ref, sin_ref, o_ref = rest
    elif kind == "scaled":
        gain_ref, o_ref = rest
    else:
        (o_ref,) = rest
    rows, tn = o_ref.shape
    w = w_ref[...]
    if kind in ("normrope", "scaled"):
        gain = gain_ref[0]
    if kind == "normrope":
        lane = lax.broadcasted_iota(jnp.int32, (cm, _HEAD_DIM), 1)
        first_half = jnp.bitwise_and(lane, _HEAD_DIM // 2 - 1) < _HEAD_DIM // 4
    for c in range(rows // cm):
        rs = slice(c * cm, (c + 1) * cm)
        acc = _dot(h_ref[rs, :], w)
        if kind == "plain":
            o_ref[rs, :] = acc.astype(o_ref.dtype)
        elif kind == "scaled":
            o_ref[rs, :] = (acc * gain[:, :1]).astype(o_ref.dtype)
        elif kind == "sigmoid":
            o_ref[rs, :] = jax.nn.sigmoid(acc).astype(o_ref.dtype)
        else:
            cos = cos_ref[rs, :]
            sin = sin_ref[rs, :]
            for hc in range(tn // _HEAD_DIM):
                cs = slice(hc * _HEAD_DIM, (hc + 1) * _HEAD_DIM)
                xh = acc[:, cs]
                ms = jnp.mean(xh * xh, axis=-1, keepdims=True)
                y = xh * lax.rsqrt(ms + _EPS) * gain
                partner = jnp.where(
                    first_half,
                    pltpu.roll(y, _HEAD_DIM - _HEAD_DIM // 4, axis=1),
                    pltpu.roll(y, _HEAD_DIM // 4, axis=1),
                )
                o_ref[rs, cs] = (y * cos + partner * sin).astype(o_ref.dtype)


def _inproj_tile(col_ranges):
    edges = [v for start_width in col_ranges for v in start_width]
    return _pick_tile(functools.reduce(math.gcd, edges), (512, 256, 128))


def _inproj(h, w, col_ranges, kind, slab_rows, gains=None, cos=None, sin=None):
    m, d = h.shape
    n = sum(width for _, width in col_ranges)
    tn = _inproj_tile(col_ranges)
    cm = _pick_tile(slab_rows, (512, 256, 128, 64, 32, 16))

    def w_block(j):
        blk, first = None, 0
        for start, width in col_ranges:
            here = start // tn + (j - first)
            blk = here if blk is None else jnp.where(j >= first, here, blk)
            first += width // tn
        return blk

    in_specs = [
        pl.BlockSpec((slab_rows, d), lambda b, j: (b, 0), pipeline_mode=pl.Buffered(1)),
        pl.BlockSpec((d, tn), lambda b, j: (0, w_block(j))),
    ]
    args = [h, w]
    vmem = slab_rows * d * 2 + 2 * d * tn * 2 + 2 * slab_rows * tn * 2 + 6 * cm * tn * 4 + (6 << 20)
    if kind in ("normrope", "scaled"):
        in_specs.append(pl.BlockSpec((1, 1, _HEAD_DIM), lambda b, j: (j, 0, 0)))
        args.append(gains)
    if kind == "normrope":
        in_specs += [
            pl.BlockSpec((slab_rows, _HEAD_DIM), lambda b, j: (0, 0), pipeline_mode=pl.Buffered(1)),
            pl.BlockSpec((slab_rows, _HEAD_DIM), lambda b, j: (0, 0), pipeline_mode=pl.Buffered(1)),
        ]
        args += [cos, sin]
        vmem += 2 * slab_rows * _HEAD_DIM * 4
    return pl.pallas_call(
        functools.partial(_inproj_kernel, kind, cm),
        grid=(m // slab_rows, n // tn),
        in_specs=in_specs,
        out_specs=pl.BlockSpec((slab_rows, tn), lambda b, j: (b, j)),
        out_shape=jax.ShapeDtypeStruct((m, n), _BF16),
        compiler_params=_params(vmem, ("arbitrary", "arbitrary")),
        name="inproj_" + kind,
    )(*args)


def _softmax_pv(s_ref, p_ref, n_ctx, vc, v):
    s = s_ref[...]
    m = jnp.max(s, axis=-1, keepdims=True)
    p = jnp.exp2(s - m)
    denom = jnp.sum(p, axis=-1, keepdims=True)
    p_ref[...] = p.astype(p_ref.dtype)
    o = _dot(p_ref[:, :n_ctx], vc) + _dot(p_ref[:, n_ctx:], v)
    return o / denom


def _attn_a_kernel(q_ref, k_ref, v_ref, kc_ref, vc_ref, o_ref, s_ref, p_ref):
    n_ctx = kc_ref.shape[0]
    k = k_ref[...]
    v = v_ref[...]
    kc = kc_ref[...]
    vc = vc_ref[...]

    ts = s_ref.shape[1]
    stages = [(slice(t * ts, (t + 1) * ts), slice(g * _HEAD_DIM, (g + 1) * _HEAD_DIM))
              for t in range(q_ref.shape[0] // ts) for g in range(_GROUPS_A)]

    def scores(n):
        q = q_ref[stages[n]]
        s_ref[n % 2, :, :n_ctx] = _dot_nt(q, kc)
        s_ref[n % 2, :, n_ctx:] = _dot_nt(q, k)

    def finish(n):
        o = _softmax_pv(s_ref.at[n % 2], p_ref.at[n % 2], n_ctx, vc, v)
        o_ref[stages[n]] = o.astype(o_ref.dtype)

    scores(0)
    for n in range(len(stages)):
        if n + 1 < len(stages):
            scores(n + 1)
        finish(n)


def _attn_a(qk, pv, ka_c, pv_c, batch, seq, ctx_len, kv_heads):
    wq = kv_heads * _GROUPS_A * _HEAD_DIM
    gw = _GROUPS_A * _HEAD_DIM
    tq = _pick_tile(seq, (512, 256, 128))
    ts = min(tq, 256)
    nq = seq // tq
    koff = wq // _HEAD_DIM
    n_keys = seq + ctx_len
    vmem = (2 * tq * gw * 2 * 2 + 4 * seq * _HEAD_DIM * 2 + 4 * ctx_len * _HEAD_DIM * 2
            + 2 * ts * n_keys * 6 + 3 * ts * n_keys * 4 + (6 << 20))
    return pl.pallas_call(
        _attn_a_kernel,
        grid=(batch, kv_heads, nq),
        in_specs=[
            pl.BlockSpec((tq, gw), lambda b, h, i: (b * nq + i, h)),
            pl.BlockSpec((seq, _HEAD_DIM), lambda b, h, i: (b, koff + h)),
            pl.BlockSpec((seq, _HEAD_DIM), lambda b, h, i: (b, h)),
            pl.BlockSpec((ctx_len, _HEAD_DIM), lambda b, h, i: (b, h)),
            pl.BlockSpec((ctx_len, _HEAD_DIM), lambda b, h, i: (b, h)),
        ],
        out_specs=pl.BlockSpec((tq, gw), lambda b, h, i: (b * nq + i, h)),
        out_shape=jax.ShapeDtypeStruct((batch * seq, wq), _BF16),
        scratch_shapes=[pltpu.VMEM((2, ts, n_keys), _F32), pltpu.VMEM((2, ts, n_keys), _BF16)],
        compiler_params=_params(vmem, ("arbitrary", "arbitrary", "arbitrary")),
        name="attn_gqa",
    )(qk, qk, pv, ka_c, pv_c)


def _window_start_row(blk, grid_rows):
    lo = _BLK_ROWS * blk - _WIN_H // 2
    if isinstance(blk, int):
        return min(max(lo, 0), grid_rows - _KEY_ROWS)
    return jnp.clip(lo, 0, grid_rows - _KEY_ROWS)


def _bias_table_kernel(grid_rows, rpb_ref, o_ref):
    h = pl.program_id(0)
    n_dr = 2 * _WIN_H - 1
    n_dc = 2 * _WIN_W - 1
    shape = (_GRID_W, 2 * _GRID_W)
    qc = lax.broadcasted_iota(jnp.int32, shape, 0)
    lane = lax.broadcasted_iota(jnp.int32, shape, 1)
    col_start = jnp.clip(qc - _WIN_W // 2, 0, _GRID_W - _WIN_W)
    neg = jnp.full(shape, _NEG_INF, _F32)

    def side(kc, on_side):
        ok = on_side & (kc >= col_start) & (kc < col_start + _WIN_W)
        return kc - qc + (_WIN_W - 1), ok

    diff_l, ok_l = side(lane, lane < _GRID_W)
    diff_r, ok_r = side(lane - _GRID_W, lane >= _GRID_W)
    left, right = [], []
    for dr in range(n_dr):
        acc_l, acc_r = neg, neg
        for dc in range(n_dc):
            val = rpb_ref[(h * n_dr + dr) * n_dc + dc] * _LOG2_E
            acc_l = jnp.where(ok_l & (diff_l == dc), val, acc_l)
            acc_r = jnp.where(ok_r & (diff_r == dc), val, acc_r)
        left.append(acc_l)
        right.append(acc_r)

    n_blk = grid_rows // _BLK_ROWS
    for variant, blk in enumerate((0, 1, n_blk - 1)):
        ws = _window_start_row(blk, grid_rows)
        for a in range(_BLK_ROWS):
            qr = _BLK_ROWS * blk + a
            rs = min(max(qr - _WIN_H // 2, 0), grid_rows - _WIN_H)
            for jp in range(_KEY_ROWS // 2):
                piece = None
                for half, table in ((0, left), (1, right)):
                    kr = ws + 2 * jp + half
                    if rs <= kr < rs + _WIN_H:
                        blk_bias = table[kr - qr + _WIN_H - 1]
                        piece = blk_bias if piece is None else jnp.maximum(piece, blk_bias)
                if piece is None:
                    piece = neg
                o_ref[0, variant, a * _GRID_W:(a + 1) * _GRID_W,
                      jp * 2 * _GRID_W:(jp + 1) * 2 * _GRID_W] = piece


def _bias_tables(rpb, grid_rows):
    heads = rpb.shape[0]
    blk = (1, 3, _BLK_ROWS * _GRID_W, _KEY_ROWS * _GRID_W)
    return pl.pallas_call(
        functools.partial(_bias_table_kernel, grid_rows),
        grid=(heads,),
        in_specs=[pl.BlockSpec(memory_space=pltpu.SMEM)],
        out_specs=pl.BlockSpec(blk, lambda h: (h, 0, 0, 0)),
        out_shape=jax.ShapeDtypeStruct((heads,) + blk[1:], _F32),
        compiler_params=_params(3 * blk[1] * blk[2] * blk[3] * 4 + (8 << 20), ("arbitrary",)),
        name="nbr_bias_tables",
    )(rpb.reshape(-1))


def _attn_b_kernel(grid_rows, q_ref, k_ref, v_ref, kc_ref, vc_ref, bias_ref, o_ref, s_ref, p_ref):
    n_ctx = kc_ref.shape[0]
    n_blk = grid_rows // _BLK_ROWS
    tq = _BLK_ROWS * _GRID_W
    span = _KEY_ROWS * _GRID_W
    kc = kc_ref[...]
    vc = vc_ref[...]

    def window(i):
        start = _window_start_row(i, grid_rows) * _GRID_W
        return slice(start, start + span)

    def scores(i):
        q = q_ref[i * tq:(i + 1) * tq, :]
        variant = 0 if i == 0 else (2 if i == n_blk - 1 else 1)
        s_ref[i % 2, :, :n_ctx] = _dot_nt(q, kc)
        s_ref[i % 2, :, n_ctx:] = _dot_nt(q, k_ref[window(i), :]) + bias_ref[0, variant]

    def finish(i):
        o = _softmax_pv(s_ref.at[i % 2], p_ref.at[i % 2], n_ctx, vc, v_ref[window(i), :])
        o_ref[i * tq:(i + 1) * tq, :] = o.astype(o_ref.dtype)

    scores(0)
    for i in range(n_blk):
        if i + 1 < n_blk:
            scores(i + 1)
        finish(i)


def _attn_b(pv, pv_c, bias, batch, seq, ctx_len, heads, q_off, k_off, v_off, kc_off, vc_off):
    grid_rows = seq // _GRID_W
    tq = _BLK_ROWS * _GRID_W
    span = _KEY_ROWS * _GRID_W
    n_keys = span + ctx_len
    vmem = (4 * 2 * seq * _HEAD_DIM * 2 + 4 * ctx_len * _HEAD_DIM * 2
            + 2 * 3 * tq * span * 4 + 2 * tq * n_keys * 6 + 3 * tq * n_keys * 4 + (6 << 20))
    return pl.pallas_call(
        functools.partial(_attn_b_kernel, grid_rows),
        grid=(heads, batch),
        in_specs=[
            pl.BlockSpec((seq, _HEAD_DIM), lambda h, b: (b, q_off + h)),
            pl.BlockSpec((seq, _HEAD_DIM), lambda h, b: (b, k_off + h)),
            pl.BlockSpec((seq, _HEAD_DIM), lambda h, b: (b, v_off + h)),
            pl.BlockSpec((ctx_len, _HEAD_DIM), lambda h, b: (b, kc_off + h)),
            pl.BlockSpec((ctx_len, _HEAD_DIM), lambda h, b: (b, vc_off + h)),
            pl.BlockSpec((1, 3, tq, span), lambda h, b: (h, 0, 0, 0)),
        ],
        out_specs=pl.BlockSpec((seq, _HEAD_DIM), lambda h, b: (b, h)),
        out_shape=jax.ShapeDtypeStruct((batch * seq, heads * _HEAD_DIM), _BF16),
        scratch_shapes=[pltpu.VMEM((2, tq, n_keys), _F32), pltpu.VMEM((2, tq, n_keys), _BF16)],
        compiler_params=_params(vmem, ("arbitrary", "arbitrary")),
        name="attn_nbr",
    )(pv, pv, pv, pv_c, pv_c, bias)


def _merge_kernel(oa_ref, ob_ref, wa_ref, wb_ref, ga_ref, gb_ref, y_ref):
    ya = _dot(oa_ref[...], wa_ref[...])
    yb = _dot(ob_ref[...], wb_ref[...])
    y = ga_ref[...].astype(_F32) * ya + gb_ref[...].astype(_F32) * yb
    y_ref[...] = y.astype(y_ref.dtype)


def _merge(o_a, o_b, w_a, w_b, gates, d_model):
    m, ka = o_a.shape
    kb = o_b.shape[1]
    tm = _pick_tile(m, (1024, 512, 256, 128))
    tn = _pick_tile(d_model, (1024, 512, 256, 128))
    nj = d_model // tn
    vmem = 2 * (tm * (ka + kb) * 2 + (ka + kb) * tn * 2 + 3 * tm * tn * 2) + 4 * tm * tn * 4 + (6 << 20)
    return pl.pallas_call(
        _merge_kernel,
        grid=(m // tm, nj),
        in_specs=[
            pl.BlockSpec((tm, ka), lambda i, j: (i, 0)),
            pl.BlockSpec((tm, kb), lambda i, j: (i, 0)),
            pl.BlockSpec((ka, tn), lambda i, j: (0, j)),
            pl.BlockSpec((kb, tn), lambda i, j: (0, j)),
            pl.BlockSpec((tm, tn), lambda i, j: (i, j)),
            pl.BlockSpec((tm, tn), lambda i, j: (i, nj + j)),
        ],
        out_specs=pl.BlockSpec((tm, tn), lambda i, j: (i, j)),
        out_shape=jax.ShapeDtypeStruct((m, d_model), _BF16),
        compiler_params=_params(vmem, ("arbitrary", "arbitrary")),
        name="branch_merge",
    )(o_a, o_b, w_a, w_b, gates, gates)


def _mm_residual_kernel(a_ref, w_ref, res_ref, gate_ref, o_ref):
    acc = _dot(a_ref[...], w_ref[...])
    o_ref[...] = res_ref[...] + gate_ref[0] * acc


def _mm_residual(a, w, res, gate, rows_per_group, tm_cands, tn_cands, name):
    m, k = a.shape
    n = w.shape[1]
    tm = _pick_tile(rows_per_group, tm_cands)
    tn = _pick_tile(n, tn_cands)
    tiles_per_group = rows_per_group // tm
    vmem = 2 * (tm * k * 2 + k * tn * 2 + 2 * tm * tn * 4) + 2 * tm * tn * 4 + (6 << 20)
    return pl.pallas_call(
        _mm_residual_kernel,
        grid=(m // tm, n // tn),
        in_specs=[
            pl.BlockSpec((tm, k), lambda i, j: (i, 0)),
            pl.BlockSpec((k, tn), lambda i, j: (0, j)),
            pl.BlockSpec((tm, tn), lambda i, j: (i, j)),
            pl.BlockSpec((1, 1, tn), lambda i, j: (i // tiles_per_group, 0, j)),
        ],
        out_specs=pl.BlockSpec((tm, tn), lambda i, j: (i, j)),
        out_shape=jax.ShapeDtypeStruct((m, n), _F32),
        compiler_params=_params(vmem, ("arbitrary", "arbitrary")),
        name=name,
    )(a, w, res, gate)


def _ffn_up_kernel(cm, h_ref, wg_ref, wv_ref, cwg_ref, cwv_ref, cbg_ref, cbv_ref, o_ref):
    rows, tn = o_ref.shape
    n_chunks = rows // cm
    wg = wg_ref[...]
    wv = wv_ref[...]
    ug, uv = [], []
    for c in range(n_chunks):
        hs = h_ref[c * cm:(c + 1) * cm, :]
        ug.append(_dot(hs, wg))
        uv.append(_dot(hs, wv))
    groups = cm // _SUBLANES
    sub = lax.broadcasted_iota(jnp.int32, (groups, _SUBLANES, tn), 1)
    zero_group = jnp.zeros((1, _SUBLANES, tn), _F32)

    def rotations(u):
        u3 = [x.reshape(groups, _SUBLANES, tn) for x in u]
        down = [pltpu.roll(x, 1, axis=1) for x in u3]
        up = [pltpu.roll(x, _SUBLANES - 1, axis=1) for x in u3]
        return u3, down, up

    def conv(rot, c, cw_ref, cb_ref):
        u3, down, up = rot
        before = down[c - 1][groups - 1:] if c > 0 else zero_group
        after = up[c + 1][:1] if c + 1 < n_chunks else zero_group
        prev = jnp.where(sub == 0, jnp.concatenate([before, down[c][:groups - 1]], axis=0), down[c])
        nxt = jnp.where(sub == _SUBLANES - 1, jnp.concatenate([up[c][1:], after], axis=0), up[c])
        out = prev * cw_ref[0:1, :] + u3[c] * cw_ref[1:2, :] + nxt * cw_ref[2:3, :] + cb_ref[...]
        return out.reshape(cm, tn)

    rot_g = rotations(ug)
    rot_v = rotations(uv)
    for c in range(n_chunks):
        gate = conv(rot_g, c, cwg_ref, cbg_ref)
        val = conv(rot_v, c, cwv_ref, cbv_ref)
        o_ref[c * cm:(c + 1) * cm, :] = (gate * jax.nn.sigmoid(gate) * val).astype(o_ref.dtype)


def _ffn_up(h2, w_up, conv_w, conv_b, seq):
    m, d = h2.shape
    d_ff = w_up.shape[1] // 2
    tn = _pick_tile(d_ff, (256, 128))
    nj = d_ff // tn
    cm = _pick_tile(seq, (512, 256, 128, 64, 32, 16))
    taps = conv_w.shape[0]
    vmem = seq * d * 2 + 2 * 2 * d * tn * 2 + 2 * seq * tn * 2 + 6 * seq * tn * 4 + (6 << 20)
    return pl.pallas_call(
        functools.partial(_ffn_up_kernel, cm),
        grid=(m // seq, nj),
        in_specs=[
            pl.BlockSpec((seq, d), lambda b, j: (b, 0), pipeline_mode=pl.Buffered(1)),
            pl.BlockSpec((d, tn), lambda b, j: (0, j)),
            pl.BlockSpec((d, tn), lambda b, j: (0, nj + j)),
            pl.BlockSpec((taps, tn), lambda b, j: (0, j)),
            pl.BlockSpec((taps, tn), lambda b, j: (0, nj + j)),
            pl.BlockSpec((1, tn), lambda b, j: (0, j)),
            pl.BlockSpec((1, tn), lambda b, j: (0, nj + j)),
        ],
        out_specs=pl.BlockSpec((seq, tn), lambda b, j: (b, j)),
        out_shape=jax.ShapeDtypeStruct((m, d_ff), _BF16),
        compiler_params=_params(vmem, ("arbitrary", "arbitrary")),
        name="ffn_up_conv_gate",
    )(h2, w_up, w_up, conv_w, conv_w, conv_b, conv_b)


def kernel(x, c, ctx, c_ctx, ada_w, ada_b, norm1_g, w_in, qnorm_a, knorm_a, rpb_b,
           w_br_a, w_br_b, w_out, norm2_g, w_up, conv_w, conv_b, w_down, final_g):
    batch, seq, d = x.shape
    ctx_len = ctx.shape[1]
    depth = ada_w.shape[0]
    wa_q = w_br_a.shape[1]
    wb = w_br_b.shape[1]
    wa_kv = (w_in.shape[2] - wa_q - 3 * wb - 2 * d) // 2
    kv_heads = wa_kv // _HEAD_DIM
    heads_b = wb // _HEAD_DIM
    assert depth == 1, "the context-stream update between layers is not implemented"
    assert wa_q == kv_heads * _GROUPS_A * _HEAD_DIM and seq % (_BLK_ROWS * _GRID_W) == 0
    assert seq // _GRID_W >= _KEY_ROWS and qnorm_a.shape[-1] == _HEAD_DIM
    grid_rows = seq // _GRID_W
    m = batch * seq
    xf = x.reshape(m, d)
    ctxf = ctx.reshape(batch * ctx_len, d)

    pad_rows = -(batch + 1) % 16
    cc = jnp.concatenate([c, c_ctx[None, :], jnp.zeros((pad_rows, d), _F32)], axis=0)
    mod = _modulation(cc, ada_w[0], ada_b[0][None, :])
    sh1, sc1, g1, sh2, sc2, g2 = [mod[:batch, i * d:(i + 1) * d][:, None, :] for i in range(6)]
    csh1, csc1 = [mod[batch:batch + 1, i * d:(i + 1) * d][:, None, :] for i in range(2)]

    w_in16 = w_in[0].astype(_BF16)
    n_in = w_in16.shape[1]
    o_k = wa_q
    o_v = o_k + wa_kv
    o_qb = o_v + wa_kv
    o_kb = o_qb + wb
    o_g = o_qb + 3 * wb

    g1n = norm1_g[0][None, :]
    h = _rmsnorm(xf, g1n, sc1, sh1, seq, _BF16)
    hc = _rmsnorm(ctxf, g1n, csc1, csh1, batch * ctx_len, _BF16)

    cos_t, sin_t = _rope_tables(seq)
    qk_cols = [(0, o_v)]
    tn_qk = _inproj_tile(qk_cols)
    assert wa_q % tn_qk == 0
    gain_q = jnp.broadcast_to(qnorm_a[0] * _QK_SCALE, (wa_q // tn_qk, 1, _HEAD_DIM))
    gain_k = jnp.broadcast_to(knorm_a[0], (wa_kv // tn_qk, 1, _HEAD_DIM))
    qk = _inproj(h, w_in16, qk_cols, "normrope", seq, jnp.concatenate([gain_q, gain_k], axis=0), cos_t, sin_t)
    pv_cols = [(o_v, o_g - o_v)]
    tn_pv = _inproj_tile(pv_cols)
    assert wa_kv % tn_pv == 0 and wb % tn_pv == 0
    pv_tile = jnp.arange((o_g - o_v) // tn_pv)
    is_qb = (pv_tile >= wa_kv // tn_pv) & (pv_tile < (wa_kv + wb) // tn_pv)
    pv_scale = jnp.broadcast_to(jnp.where(is_qb, _QK_SCALE, 1.0).astype(_F32)[:, None, None],
                                (pv_tile.shape[0], 1, _HEAD_DIM))
    pv = _inproj(h, w_in16, pv_cols, "scaled", seq, pv_scale)
    gates = _inproj(h, w_in16, [(o_g, n_in - o_g)], "sigmoid", seq)
    c_rows = batch * ctx_len
    kc_cols = [(o_k, wa_kv)]
    gain_kc = jnp.broadcast_to(knorm_a[0], (wa_kv // _inproj_tile(kc_cols), 1, _HEAD_DIM))
    ka_c = _inproj(hc, w_in16, kc_cols, "normrope", c_rows, gain_kc,
                   jnp.ones((c_rows, _HEAD_DIM), _F32), jnp.zeros((c_rows, _HEAD_DIM), _F32))
    pv_c = _inproj(hc, w_in16, [(o_v, wa_kv), (o_kb, 2 * wb)], "plain", c_rows)

    o_a = _attn_a(qk, pv, ka_c, pv_c, batch, seq, ctx_len, kv_heads)
    bias = _bias_tables(rpb_b[0], grid_rows)
    o_b = _attn_b(pv, pv_c, bias, batch, seq, ctx_len, heads_b,
                  q_off=kv_heads, k_off=kv_heads + heads_b, v_off=kv_heads + 2 * heads_b,
                  kc_off=kv_heads, vc_off=kv_heads + heads_b)

    y = _merge(o_a, o_b, w_br_a[0].astype(_BF16), w_br_b[0].astype(_BF16), gates, d)
    x1 = _mm_residual(y, w_out[0].astype(_BF16), xf, g1, seq,
                      (1024, 512, 256, 128), (1024, 512, 256, 128), "out_proj_residual")

    h2 = _rmsnorm(x1, norm2_g[0][None, :], sc2, sh2, seq, _BF16)
    t = _ffn_up(h2, w_up[0].astype(_BF16), conv_w[0], conv_b[0][None, :], seq)
    x2 = _mm_residual(t, w_down[0].astype(_BF16), x1, g2, seq,
                      (512, 256, 128), (512, 256, 128), "ffn_down_residual")

    out = _rmsnorm(x2, final_g[None, :], None, None, seq, _F32)
    return out.reshape(batch, seq, d)
```

```python
import functools
import math

import jax
import jax.numpy as jnp
from jax import lax
from jax.experimental import pallas as pl
from jax.experimental.pallas import tpu as pltpu

_HEAD_DIM = 128
_GRID_W = 64
_WIN_H = 8
_WIN_W = 16
_GROUPS_A = 4
_ROPE_THETA = 10000.0
_EPS = 1e-6
_BLK_ROWS = 8
_KEY_ROWS = 2 * _BLK_ROWS
_SUBLANES = 8
_V7X_VMEM_BYTES = 64 * 1024 * 1024
_VMEM_CAP = _V7X_VMEM_BYTES - 6 * 1024 * 1024
_NEG_INF = float("-inf")
_LOG2_E = 1.4426950408889634
_QK_SCALE = _LOG2_E * _HEAD_DIM ** -0.5

_BF16 = jnp.bfloat16
_F32 = jnp.float32


def _params(vmem_bytes, semantics):
    limit = int(min(max(vmem_bytes, 32 * 1024 * 1024), _VMEM_CAP))
    return pltpu.CompilerParams(dimension_semantics=semantics, vmem_limit_bytes=limit)


def _dot(a, b):
    return jnp.dot(a, b, preferred_element_type=_F32)


def _dot_nt(a, b):
    return lax.dot_general(a, b, (((1,), (1,)), ((), ())), preferred_element_type=_F32)


def _pick_tile(n, candidates):
    for t in candidates:
        if n % t == 0:
            return t
    raise ValueError(f"no tile in {candidates} divides {n}")


def _cast_plan(weights, grid):
    n_steps = math.prod(grid)

    def step(*ids):
        s = ids[0]
        for n, i in zip(grid[1:], ids[1:]):
            s = s * n + i
        return s

    in_specs, out_specs, out_shapes, vmem = [], [], [], 0
    for w in weights:
        r, c = w.shape
        br = next(b for b in range(16, r + 1, 16) if r % b == 0 and r // b <= n_steps)
        last = r // br - 1

        def index(*ids, last=last):
            return jnp.minimum(step(*ids), last), 0

        in_specs.append(pl.BlockSpec((br, c), index))
        out_specs.append(pl.BlockSpec((br, c), index))
        out_shapes.append(jax.ShapeDtypeStruct((r, c), _BF16))
        vmem += 2 * br * c * (4 + 2) + br * c * 4
    return in_specs, out_specs, out_shapes, vmem


def _run_casts(srcs, dsts):
    for src, dst in zip(srcs, dsts):
        dst[...] = src[...].astype(dst.dtype)


def _mod_kernel(c_ref, w_ref, b_ref, o_ref):
    c = c_ref[...]
    s = (c * jax.nn.sigmoid(c)).astype(_BF16)
    o_ref[...] = _dot(s, w_ref[...].astype(_BF16)) + b_ref[...]


def _modulation(cc, ada_w, ada_b):
    rows, d = cc.shape
    n = ada_w.shape[1]
    tn = _pick_tile(n, (512, 256, 128))
    vmem = 2 * (d * tn * 4) + d * tn * 2 + 4 * rows * d * 4 + (4 << 20)
    return pl.pallas_call(
        _mod_kernel,
        grid=(n // tn,),
        in_specs=[
            pl.BlockSpec((rows, d), lambda j: (0, 0)),
            pl.BlockSpec((d, tn), lambda j: (0, j)),
            pl.BlockSpec((1, tn), lambda j: (0, j)),
        ],
        out_specs=pl.BlockSpec((rows, tn), lambda j: (0, j)),
        out_shape=jax.ShapeDtypeStruct((rows, n), _F32),
        compiler_params=_params(vmem, ("arbitrary",)),
        name="adaln_mod",
    )(cc, ada_w, ada_b)


def _norm_kernel(modulate, x_ref, g_ref, *rest):
    if modulate:
        sc_ref, sh_ref, o_ref = rest
    else:
        (o_ref,) = rest
    x = x_ref[...]
    ms = jnp.mean(x * x, axis=-1, keepdims=True)
    y = x * lax.rsqrt(ms + _EPS) * g_ref[...]
    if modulate:
        y = y * (1.0 + sc_ref[0]) + sh_ref[0]
    o_ref[...] = y.astype(o_ref.dtype)


def _rmsnorm(x, g, sc, sh, rows_per_group, out_dtype):
    m, d = x.shape
    tm = _pick_tile(rows_per_group, (512, 256, 128, 64, 32, 16, 8))
    tiles_per_group = rows_per_group // tm
    modulate = sc is not None
    in_specs = [pl.BlockSpec((tm, d), lambda i: (i, 0)), pl.BlockSpec((1, d), lambda i: (0, 0))]
    args = [x, g]
    if modulate:
        mod_spec = pl.BlockSpec((1, 1, d), lambda i: (i // tiles_per_group, 0, 0))
        in_specs += [mod_spec, mod_spec]
        args += [sc, sh]
    vmem = 2 * tm * d * 4 + 2 * tm * d * jnp.dtype(out_dtype).itemsize + 3 * tm * d * 4 + (4 << 20)
    return pl.pallas_call(
        functools.partial(_norm_kernel, modulate),
        grid=(m // tm,),
        in_specs=in_specs,
        out_specs=pl.BlockSpec((tm, d), lambda i: (i, 0)),
        out_shape=jax.ShapeDtypeStruct((m, d), out_dtype),
        compiler_params=_params(vmem, ("arbitrary",)),
        name="rmsnorm_mod" if modulate else "rmsnorm",
    )(*args)


def _rope_table_kernel(freq_ref, cos_ref, sin_ref):
    shape = cos_ref.shape
    t = lax.broadcasted_iota(jnp.int32, shape, 0)
    lane = lax.broadcasted_iota(jnp.int32, shape, 1)
    row = lax.shift_right_logical(t, _GRID_W.bit_length() - 1)
    col = jnp.bitwise_and(t, _GRID_W - 1)
    pos = jnp.where(lane < _HEAD_DIM // 2, row, col).astype(_F32)
    ang = pos * freq_ref[...]
    first_half = jnp.bitwise_and(lane, _HEAD_DIM // 2 - 1) < _HEAD_DIM // 4
    cos_ref[...] = jnp.cos(ang)
    sin_ref[...] = jnp.where(first_half, -jnp.sin(ang), jnp.sin(ang))


def _rope_tables(seq):
    r = _HEAD_DIM // 4
    freqs = _ROPE_THETA ** (-jnp.arange(r, dtype=_F32) / r)
    freq_lanes = jnp.tile(freqs, _HEAD_DIM // r)[None, :]
    out = jax.ShapeDtypeStruct((seq, _HEAD_DIM), _F32)
    return pl.pallas_call(
        _rope_table_kernel,
        out_shape=(out, out),
        name="rope_tables",
    )(freq_lanes)


def _inproj_kernel(kind, cm, h_ref, w_ref, *rest):
    if kind == "normrope":
        gain_ref, cos_ref, sin_ref, o_ref = rest
    elif kind == "scaled":
        gain_ref, o_ref = rest
    else:
        (o_ref,) = rest
    rows, tn = o_ref.shape
    w = w_ref[...]
    if kind in ("normrope", "scaled"):
        gain = gain_ref[0]
    if kind == "normrope":
        lane = lax.broadcasted_iota(jnp.int32, (cm, _HEAD_DIM), 1)
        first_half = jnp.bitwise_and(lane, _HEAD_DIM // 2 - 1) < _HEAD_DIM // 4
    for c in range(rows // cm):
        rs = slice(c * cm, (c + 1) * cm)
        acc = _dot(h_ref[rs, :], w)
        if kind == "plain":
            o_ref[rs, :] = acc.astype(o_ref.dtype)
        elif kind == "scaled":
            o_ref[rs, :] = (acc * gain[:, :1]).astype(o_ref.dtype)
        elif kind == "sigmoid":
            o_ref[rs, :] = jax.nn.sigmoid(acc).astype(o_ref.dtype)
        else:
            cos = cos_ref[rs, :]
            sin = sin_ref[rs, :]
            for hc in range(tn // _HEAD_DIM):
                cs = slice(hc * _HEAD_DIM, (hc + 1) * _HEAD_DIM)
                xh = acc[:, cs]
                ms = jnp.mean(xh * xh, axis=-1, keepdims=True)
                y = xh * lax.rsqrt(ms + _EPS) * gain
                partner = jnp.where(
                    first_half,
                    pltpu.roll(y, _HEAD_DIM - _HEAD_DIM // 4, axis=1),
                    pltpu.roll(y, _HEAD_DIM // 4, axis=1),
                )
                o_ref[rs, cs] = (y * cos + partner * sin).astype(o_ref.dtype)


def _inproj_tile(col_ranges):
    edges = [v for start_width in col_ranges for v in start_width]
    return _pick_tile(functools.reduce(math.gcd, edges), (512, 256, 128))


def _inproj(h, w, col_ranges, kind, slab_rows, gains=None, cos=None, sin=None):
    m, d = h.shape
    n = sum(width for _, width in col_ranges)
    tn = _inproj_tile(col_ranges)
    cm = _pick_tile(slab_rows, (512, 256, 128, 64, 32, 16))

    def w_block(j):
        blk, first = None, 0
        for start, width in col_ranges:
            here = start // tn + (j - first)
            blk = here if blk is None else jnp.where(j >= first, here, blk)
            first += width // tn
        return blk

    in_specs = [
        pl.BlockSpec((slab_rows, d), lambda b, j: (b, 0), pipeline_mode=pl.Buffered(1)),
        pl.BlockSpec((d, tn), lambda b, j: (0, w_block(j))),
    ]
    args = [h, w]
    vmem = slab_rows * d * 2 + 2 * d * tn * 2 + 2 * slab_rows * tn * 2 + 6 * cm * tn * 4 + (6 << 20)
    if kind in ("normrope", "scaled"):
        in_specs.append(pl.BlockSpec((1, 1, _HEAD_DIM), lambda b, j: (j, 0, 0)))
        args.append(gains)
    if kind == "normrope":
        in_specs += [
            pl.BlockSpec((slab_rows, _HEAD_DIM), lambda b, j: (0, 0), pipeline_mode=pl.Buffered(1)),
            pl.BlockSpec((slab_rows, _HEAD_DIM), lambda b, j: (0, 0), pipeline_mode=pl.Buffered(1)),
        ]
        args += [cos, sin]
        vmem += 2 * slab_rows * _HEAD_DIM * 4
    return pl.pallas_call(
        functools.partial(_inproj_kernel, kind, cm),
        grid=(m // slab_rows, n // tn),
        in_specs=in_specs,
        out_specs=pl.BlockSpec((slab_rows, tn), lambda b, j: (b, j)),
        out_shape=jax.ShapeDtypeStruct((m, n), _BF16),
        compiler_params=_params(vmem, ("arbitrary", "arbitrary")),
        name="inproj_" + kind,
    )(*args)


def _softmax_pv(s_ref, p_ref, n_ctx, vc, v):
    s = s_ref[...]
    m = jnp.max(s, axis=-1, keepdims=True)
    p = jnp.exp2(s - m)
    denom = jnp.sum(p, axis=-1, keepdims=True)
    p_ref[...] = p.astype(p_ref.dtype)
    o = _dot(p_ref[:, :n_ctx], vc) + _dot(p_ref[:, n_ctx:], v)
    return o / denom


def _attn_a_kernel(n_cast, q_ref, k_ref, v_ref, kc_ref, vc_ref, *rest):
    cast_srcs, o_ref, cast_dsts = rest[:n_cast], rest[n_cast], rest[n_cast + 1:2 * n_cast + 1]
    s_ref, p_ref = rest[2 * n_cast + 1:]
    _run_casts(cast_srcs, cast_dsts)
    n_ctx = kc_ref.shape[0]
    k = k_ref[...]
    v = v_ref[...]
    kc = kc_ref[...]
    vc = vc_ref[...]

    ts = s_ref.shape[1]
    stages = [(slice(t * ts, (t + 1) * ts), slice(g * _HEAD_DIM, (g + 1) * _HEAD_DIM))
              for t in range(q_ref.shape[0] // ts) for g in range(_GROUPS_A)]

    def scores(n):
        q = q_ref[stages[n]]
        s_ref[n % 2, :, :n_ctx] = _dot_nt(q, kc)
        s_ref[n % 2, :, n_ctx:] = _dot_nt(q, k)

    def finish(n):
        o = _softmax_pv(s_ref.at[n % 2], p_ref.at[n % 2], n_ctx, vc, v)
        o_ref[stages[n]] = o.astype(o_ref.dtype)

    scores(0)
    for n in range(len(stages)):
        if n + 1 < len(stages):
            scores(n + 1)
        finish(n)


def _attn_a(qk, pv, ka_c, pv_c, batch, seq, ctx_len, kv_heads, casts):
    wq = kv_heads * _GROUPS_A * _HEAD_DIM
    gw = _GROUPS_A * _HEAD_DIM
    tq = _pick_tile(seq, (512, 256, 128))
    ts = min(tq, 256)
    nq = seq // tq
    koff = wq // _HEAD_DIM
    n_keys = seq + ctx_len
    vmem = (2 * tq * gw * 2 * 2 + 4 * seq * _HEAD_DIM * 2 + 4 * ctx_len * _HEAD_DIM * 2
            + 2 * ts * n_keys * 6 + 3 * ts * n_keys * 4 + (6 << 20))
    grid = (batch, kv_heads, nq)
    cast_in, cast_out, cast_shapes, cast_vmem = _cast_plan(casts, grid)
    return pl.pallas_call(
        functools.partial(_attn_a_kernel, len(casts)),
        grid=grid,
        in_specs=[
            pl.BlockSpec((tq, gw), lambda b, h, i: (b * nq + i, h)),
            pl.BlockSpec((seq, _HEAD_DIM), lambda b, h, i: (b, koff + h)),
            pl.BlockSpec((seq, _HEAD_DIM), lambda b, h, i: (b, h)),
            pl.BlockSpec((ctx_len, _HEAD_DIM), lambda b, h, i: (b, h)),
            pl.BlockSpec((ctx_len, _HEAD_DIM), lambda b, h, i: (b, h)),
        ] + cast_in,
        out_specs=[pl.BlockSpec((tq, gw), lambda b, h, i: (b * nq + i, h))] + cast_out,
        out_shape=[jax.ShapeDtypeStruct((batch * seq, wq), _BF16)] + cast_shapes,
        scratch_shapes=[pltpu.VMEM((2, ts, n_keys), _F32), pltpu.VMEM((2, ts, n_keys), _BF16)],
        compiler_params=_params(vmem + cast_vmem, ("arbitrary", "arbitrary", "arbitrary")),
        name="attn_gqa",
    )(qk, qk, pv, ka_c, pv_c, *casts)


def _window_start_row(blk, grid_rows):
    lo = _BLK_ROWS * blk - _WIN_H // 2
    if isinstance(blk, int):
        return min(max(lo, 0), grid_rows - _KEY_ROWS)
    return jnp.clip(lo, 0, grid_rows - _KEY_ROWS)


def _bias_table_kernel(grid_rows, rpb_ref, o_ref):
    h = pl.program_id(0)
    n_dr = 2 * _WIN_H - 1
    n_dc = 2 * _WIN_W - 1
    shape = (_GRID_W, 2 * _GRID_W)
    qc = lax.broadcasted_iota(jnp.int32, shape, 0)
    lane = lax.broadcasted_iota(jnp.int32, shape, 1)
    col_start = jnp.clip(qc - _WIN_W // 2, 0, _GRID_W - _WIN_W)
    neg = jnp.full(shape, _NEG_INF, _F32)

    def side(kc, on_side):
        ok = on_side & (kc >= col_start) & (kc < col_start + _WIN_W)
        return kc - qc + (_WIN_W - 1), ok

    diff_l, ok_l = side(lane, lane < _GRID_W)
    diff_r, ok_r = side(lane - _GRID_W, lane >= _GRID_W)
    left, right = [], []
    for dr in range(n_dr):
        acc_l, acc_r = neg, neg
        for dc in range(n_dc):
            val = rpb_ref[(h * n_dr + dr) * n_dc + dc] * _LOG2_E
            acc_l = jnp.where(ok_l & (diff_l == dc), val, acc_l)
            acc_r = jnp.where(ok_r & (diff_r == dc), val, acc_r)
        left.append(acc_l)
        right.append(acc_r)

    n_blk = grid_rows // _BLK_ROWS
    for variant, blk in enumerate((0, 1, n_blk - 1)):
        ws = _window_start_row(blk, grid_rows)
        for a in range(_BLK_ROWS):
            qr = _BLK_ROWS * blk + a
            rs = min(max(qr - _WIN_H // 2, 0), grid_rows - _WIN_H)
            for jp in range(_KEY_ROWS // 2):
                piece = None
                for half, table in ((0, left), (1, right)):
                    kr = ws + 2 * jp + half
                    if rs <= kr < rs + _WIN_H:
                        blk_bias = table[kr - qr + _WIN_H - 1]
                        piece = blk_bias if piece is None else jnp.maximum(piece, blk_bias)
                if piece is None:
                    piece = neg
                o_ref[0, variant, a * _GRID_W:(a + 1) * _GRID_W,
                      jp * 2 * _GRID_W:(jp + 1) * 2 * _GRID_W] = piece


def _bias_tables(rpb, grid_rows):
    heads = rpb.shape[0]
    blk = (1, 3, _BLK_ROWS * _GRID_W, _KEY_ROWS * _GRID_W)
    return pl.pallas_call(
        functools.partial(_bias_table_kernel, grid_rows),
        grid=(heads,),
        in_specs=[pl.BlockSpec(memory_space=pltpu.SMEM)],
        out_specs=pl.BlockSpec(blk, lambda h: (h, 0, 0, 0)),
        out_shape=jax.ShapeDtypeStruct((heads,) + blk[1:], _F32),
        compiler_params=_params(3 * blk[1] * blk[2] * blk[3] * 4 + (8 << 20), ("arbitrary",)),
        name="nbr_bias_tables",
    )(rpb.reshape(-1))


def _attn_b_kernel(grid_rows, n_cast, q_ref, k_ref, v_ref, kc_ref, vc_ref, bias_ref, *rest):
    cast_srcs, o_ref, cast_dsts = rest[:n_cast], rest[n_cast], rest[n_cast + 1:2 * n_cast + 1]
    s_ref, p_ref = rest[2 * n_cast + 1:]
    _run_casts(cast_srcs, cast_dsts)
    n_ctx = kc_ref.shape[0]
    n_blk = grid_rows // _BLK_ROWS
    tq = _BLK_ROWS * _GRID_W
    span = _KEY_ROWS * _GRID_W
    kc = kc_ref[...]
    vc = vc_ref[...]

    def window(i):
        start = _window_start_row(i, grid_rows) * _GRID_W
        return slice(start, start + span)

    def scores(i):
        q = q_ref[i * tq:(i + 1) * tq, :]
        variant = 0 if i == 0 else (2 if i == n_blk - 1 else 1)
        s_ref[i % 2, :, :n_ctx] = _dot_nt(q, kc)
        s_ref[i % 2, :, n_ctx:] = _dot_nt(q, k_ref[window(i), :]) + bias_ref[0, variant]

    def finish(i):
        o = _softmax_pv(s_ref.at[i % 2], p_ref.at[i % 2], n_ctx, vc, v_ref[window(i), :])
        o_ref[i * tq:(i + 1) * tq, :] = o.astype(o_ref.dtype)

    scores(0)
    for i in range(n_blk):
        if i + 1 < n_blk:
            scores(i + 1)
        finish(i)


def _attn_b(pv, pv_c, bias, batch, seq, ctx_len, heads, q_off, k_off, v_off, kc_off, vc_off, casts):
    grid_rows = seq // _GRID_W
    tq = _BLK_ROWS * _GRID_W
    span = _KEY_ROWS * _GRID_W
    n_keys = span + ctx_len
    vmem = (4 * 2 * seq * _HEAD_DIM * 2 + 4 * ctx_len * _HEAD_DIM * 2
            + 2 * 3 * tq * span * 4 + 2 * tq * n_keys * 6 + 3 * tq * n_keys * 4 + (6 << 20))
    grid = (heads, batch)
    cast_in, cast_out, cast_shapes, cast_vmem = _cast_plan(casts, grid)
    return pl.pallas_call(
        functools.partial(_attn_b_kernel, grid_rows, len(casts)),
        grid=grid,
        in_specs=[
            pl.BlockSpec((seq, _HEAD_DIM), lambda h, b: (b, q_off + h)),
            pl.BlockSpec((seq, _HEAD_DIM), lambda h, b: (b, k_off + h)),
            pl.BlockSpec((seq, _HEAD_DIM), lambda h, b: (b, v_off + h)),
            pl.BlockSpec((ctx_len, _HEAD_DIM), lambda h, b: (b, kc_off + h)),
            pl.BlockSpec((ctx_len, _HEAD_DIM), lambda h, b: (b, vc_off + h)),
            pl.BlockSpec((1, 3, tq, span), lambda h, b: (h, 0, 0, 0)),
        ] + cast_in,
        out_specs=[pl.BlockSpec((seq, _HEAD_DIM), lambda h, b: (b, h))] + cast_out,
        out_shape=[jax.ShapeDtypeStruct((batch * seq, heads * _HEAD_DIM), _BF16)] + cast_shapes,
        scratch_shapes=[pltpu.VMEM((2, tq, n_keys), _F32), pltpu.VMEM((2, tq, n_keys), _BF16)],
        compiler_params=_params(vmem + cast_vmem, ("arbitrary", "arbitrary")),
        name="attn_nbr",
    )(pv, pv, pv, pv_c, pv_c, bias, *casts)


def _merge_kernel(oa_ref, ob_ref, wa_ref, wb_ref, ga_ref, gb_ref, y_ref):
    ya = _dot(oa_ref[...], wa_ref[...])
    yb = _dot(ob_ref[...], wb_ref[...])
    y = ga_ref[...].astype(_F32) * ya + gb_ref[...].astype(_F32) * yb
    y_ref[...] = y.astype(y_ref.dtype)


def _merge(o_a, o_b, w_a, w_b, gates, d_model):
    m, ka = o_a.shape
    kb = o_b.shape[1]
    tm = _pick_tile(m, (1024, 512, 256, 128))
    tn = _pick_tile(d_model, (1024, 512, 256, 128))
    nj = d_model // tn
    vmem = 2 * (tm * (ka + kb) * 2 + (ka + kb) * tn * 2 + 3 * tm * tn * 2) + 4 * tm * tn * 4 + (6 << 20)
    return pl.pallas_call(
        _merge_kernel,
        grid=(m // tm, nj),
        in_specs=[
            pl.BlockSpec((tm, ka), lambda i, j: (i, 0)),
            pl.BlockSpec((tm, kb), lambda i, j: (i, 0)),
            pl.BlockSpec((ka, tn), lambda i, j: (0, j)),
            pl.BlockSpec((kb, tn), lambda i, j: (0, j)),
            pl.BlockSpec((tm, tn), lambda i, j: (i, j)),
            pl.BlockSpec((tm, tn), lambda i, j: (i, nj + j)),
        ],
        out_specs=pl.BlockSpec((tm, tn), lambda i, j: (i, j)),
        out_shape=jax.ShapeDtypeStruct((m, d_model), _BF16),
        compiler_params=_params(vmem, ("arbitrary", "arbitrary")),
        name="branch_merge",
    )(o_a, o_b, w_a, w_b, gates, gates)


def _mm_residual_kernel(a_ref, w_ref, res_ref, gate_ref, o_ref):
    acc = _dot(a_ref[...], w_ref[...])
    o_ref[...] = res_ref[...] + gate_ref[0] * acc


def _mm_residual(a, w, res, gate, rows_per_group, tm_cands, tn_cands, name):
    m, k = a.shape
    n = w.shape[1]
    tm = _pick_tile(rows_per_group, tm_cands)
    tn = _pick_tile(n, tn_cands)
    tiles_per_group = rows_per_group // tm
    vmem = 2 * (tm * k * 2 + k * tn * 2 + 2 * tm * tn * 4) + 2 * tm * tn * 4 + (6 << 20)
    return pl.pallas_call(
        _mm_residual_kernel,
        grid=(m // tm, n // tn),
        in_specs=[
            pl.BlockSpec((tm, k), lambda i, j: (i, 0)),
            pl.BlockSpec((k, tn), lambda i, j: (0, j)),
            pl.BlockSpec((tm, tn), lambda i, j: (i, j)),
            pl.BlockSpec((1, 1, tn), lambda i, j: (i // tiles_per_group, 0, j)),
        ],
        out_specs=pl.BlockSpec((tm, tn), lambda i, j: (i, j)),
        out_shape=jax.ShapeDtypeStruct((m, n), _F32),
        compiler_params=_params(vmem, ("arbitrary", "arbitrary")),
        name=name,
    )(a, w, res, gate)


def _ffn_up_kernel(cm, h_ref, wg_ref, wv_ref, cwg_ref, cwv_ref, cbg_ref, cbv_ref, o_ref):
    rows, tn = o_ref.shape
    n_chunks = rows // cm
    wg = wg_ref[...]
    wv = wv_ref[...]
    ug, uv = [], []
    for c in range(n_chunks):
        hs = h_ref[c * cm:(c + 1) * cm, :]
        ug.append(_dot(hs, wg))
        uv.append(_dot(hs, wv))
    groups = cm // _SUBLANES
    sub = lax.broadcasted_iota(jnp.int32, (groups, _SUBLANES, tn), 1)
    zero_group = jnp.zeros((1, _SUBLANES, tn), _F32)

    def rotations(u):
        u3 = [x.reshape(groups, _SUBLANES, tn) for x in u]
        down = [pltpu.roll(x, 1, axis=1) for x in u3]
        up = [pltpu.roll(x, _SUBLANES - 1, axis=1) for x in u3]
        return u3, down, up

    def conv(rot, c, cw_ref, cb_ref):
        u3, down, up = rot
        before = down[c - 1][groups - 1:] if c > 0 else zero_group
        after = up[c + 1][:1] if c + 1 < n_chunks else zero_group
        prev = jnp.where(sub == 0, jnp.concatenate([before, down[c][:groups - 1]], axis=0), down[c])
        nxt = jnp.where(sub == _SUBLANES - 1, jnp.concatenate([up[c][1:], after], axis=0), up[c])
        out = prev * cw_ref[0:1, :] + u3[c] * cw_ref[1:2, :] + nxt * cw_ref[2:3, :] + cb_ref[...]
        return out.reshape(cm, tn)

    rot_g = rotations(ug)
    rot_v = rotations(uv)
    for c in range(n_chunks):
        gate = conv(rot_g, c, cwg_ref, cbg_ref)
        val = conv(rot_v, c, cwv_ref, cbv_ref)
        o_ref[c * cm:(c + 1) * cm, :] = (gate * jax.nn.sigmoid(gate) * val).astype(o_ref.dtype)


def _ffn_up(h2, w_up, conv_w, conv_b, seq):
    m, d = h2.shape
    d_ff = w_up.shape[1] // 2
    tn = _pick_tile(d_ff, (256, 128))
    nj = d_ff // tn
    cm = _pick_tile(seq, (512, 256, 128, 64, 32, 16))
    taps = conv_w.shape[0]
    vmem = seq * d * 2 + 2 * 2 * d * tn * 2 + 2 * seq * tn * 2 + 6 * seq * tn * 4 + (6 << 20)
    return pl.pallas_call(
        functools.partial(_ffn_up_kernel, cm),
        grid=(m // seq, nj),
        in_specs=[
            pl.BlockSpec((seq, d), lambda b, j: (b, 0), pipeline_mode=pl.Buffered(1)),
            pl.BlockSpec((d, tn), lambda b, j: (0, j)),
            pl.BlockSpec((d, tn), lambda b, j: (0, nj + j)),
            pl.BlockSpec((taps, tn), lambda b, j: (0, j)),
            pl.BlockSpec((taps, tn), lambda b, j: (0, nj + j)),
            pl.BlockSpec((1, tn), lambda b, j: (0, j)),
            pl.BlockSpec((1, tn), lambda b, j: (0, nj + j)),
        ],
        out_specs=pl.BlockSpec((seq, tn), lambda b, j: (b, j)),
        out_shape=jax.ShapeDtypeStruct((m, d_ff), _BF16),
        compiler_params=_params(vmem, ("arbitrary", "arbitrary")),
        name="ffn_up_conv_gate",
    )(h2, w_up, w_up, conv_w, conv_w, conv_b, conv_b)


def kernel(x, c, ctx, c_ctx, ada_w, ada_b, norm1_g, w_in, qnorm_a, knorm_a, rpb_b,
           w_br_a, w_br_b, w_out, norm2_g, w_up, conv_w, conv_b, w_down, final_g):
    batch, seq, d = x.shape
    ctx_len = ctx.shape[1]
    depth = ada_w.shape[0]
    wa_q = w_br_a.shape[1]
    wb = w_br_b.shape[1]
    wa_kv = (w_in.shape[2] - wa_q - 3 * wb - 2 * d) // 2
    kv_heads = wa_kv // _HEAD_DIM
    heads_b = wb // _HEAD_DIM
    assert depth == 1, "the context-stream update between layers is not implemented"
    assert wa_q == kv_heads * _GROUPS_A * _HEAD_DIM and seq % (_BLK_ROWS * _GRID_W) == 0
    assert seq // _GRID_W >= _KEY_ROWS and qnorm_a.shape[-1] == _HEAD_DIM
    grid_rows = seq // _GRID_W
    m = batch * seq
    xf = x.reshape(m, d)
    ctxf = ctx.reshape(batch * ctx_len, d)

    pad_rows = -(batch + 1) % 16
    cc = jnp.concatenate([c, c_ctx[None, :], jnp.zeros((pad_rows, d), _F32)], axis=0)
    mod = _modulation(cc, ada_w[0], ada_b[0][None, :])
    sh1, sc1, g1, sh2, sc2, g2 = [mod[:batch, i * d:(i + 1) * d][:, None, :] for i in range(6)]
    csh1, csc1 = [mod[batch:batch + 1, i * d:(i + 1) * d][:, None, :] for i in range(2)]

    w_in16 = w_in[0].astype(_BF16)
    n_in = w_in16.shape[1]
    o_k = wa_q
    o_v = o_k + wa_kv
    o_qb = o_v + wa_kv
    o_kb = o_qb + wb
    o_g = o_qb + 3 * wb

    g1n = norm1_g[0][None, :]
    h = _rmsnorm(xf, g1n, sc1, sh1, seq, _BF16)
    hc = _rmsnorm(ctxf, g1n, csc1, csh1, batch * ctx_len, _BF16)

    cos_t, sin_t = _rope_tables(seq)
    qk_cols = [(0, o_v)]
    tn_qk = _inproj_tile(qk_cols)
    assert wa_q % tn_qk == 0
    gain_q = jnp.broadcast_to(qnorm_a[0] * _QK_SCALE, (wa_q // tn_qk, 1, _HEAD_DIM))
    gain_k = jnp.broadcast_to(knorm_a[0], (wa_kv // tn_qk, 1, _HEAD_DIM))
    qk = _inproj(h, w_in16, qk_cols, "normrope", seq, jnp.concatenate([gain_q, gain_k], axis=0), cos_t, sin_t)
    pv_cols = [(o_v, o_g - o_v)]
    tn_pv = _inproj_tile(pv_cols)
    assert wa_kv % tn_pv == 0 and wb % tn_pv == 0
    pv_tile = jnp.arange((o_g - o_v) // tn_pv)
    is_qb = (pv_tile >= wa_kv // tn_pv) & (pv_tile < (wa_kv + wb) // tn_pv)
    pv_scale = jnp.broadcast_to(jnp.where(is_qb, _QK_SCALE, 1.0).astype(_F32)[:, None, None],
                                (pv_tile.shape[0], 1, _HEAD_DIM))
    pv = _inproj(h, w_in16, pv_cols, "scaled", seq, pv_scale)
    gates = _inproj(h, w_in16, [(o_g, n_in - o_g)], "sigmoid", seq)
    c_rows = batch * ctx_len
    kc_cols = [(o_k, wa_kv)]
    gain_kc = jnp.broadcast_to(knorm_a[0], (wa_kv // _inproj_tile(kc_cols), 1, _HEAD_DIM))
    ka_c = _inproj(hc, w_in16, kc_cols, "normrope", c_rows, gain_kc,
                   jnp.ones((c_rows, _HEAD_DIM), _F32), jnp.zeros((c_rows, _HEAD_DIM), _F32))
    pv_c = _inproj(hc, w_in16, [(o_v, wa_kv), (o_kb, 2 * wb)], "plain", c_rows)

    o_a, w_up16, w_down16 = _attn_a(qk, pv, ka_c, pv_c, batch, seq, ctx_len, kv_heads,
                                    casts=[w_up[0], w_down[0]])
    bias = _bias_tables(rpb_b[0], grid_rows)
    o_b, w_bra16, w_brb16, w_out16 = _attn_b(
        pv, pv_c, bias, batch, seq, ctx_len, heads_b,
        q_off=kv_heads, k_off=kv_heads + heads_b, v_off=kv_heads + 2 * heads_b,
        kc_off=kv_heads, vc_off=kv_heads + heads_b, casts=[w_br_a[0], w_br_b[0], w_out[0]])

    y = _merge(o_a, o_b, w_bra16, w_brb16, gates, d)
    x1 = _mm_residual(y, w_out16, xf, g1, seq,
                      (1024, 512, 256, 128), (1024, 512, 256, 128), "out_proj_residual")

    h2 = _rmsnorm(x1, norm2_g[0][None, :], sc2, sh2, seq, _BF16)
    t = _ffn_up(h2, w_up16, conv_w[0], conv_b[0][None, :], seq)
    x2 = _mm_residual(t, w_down16, x1, g2, seq,
                      (512, 256, 128), (512, 256, 128), "ffn_down_residual")

    out = _rmsnorm(x2, final_g[None, :], None, None, seq, _F32)
    return out.reshape(batch, seq, d)
```

```python
import functools
import math

import jax
import jax.numpy as jnp
from jax import lax
from jax.experimental import pallas as pl
from jax.experimental.pallas import tpu as pltpu

_HEAD_DIM = 128
_GRID_W = 64
_WIN_H = 8
_WIN_W = 16
_GROUPS_A = 4
_ROPE_THETA = 10000.0
_EPS = 1e-6
_BLK_ROWS = 8
_KEY_ROWS = 2 * _BLK_ROWS
_SUBLANES = 8
_V7X_VMEM_BYTES = 64 * 1024 * 1024
_VMEM_CAP = _V7X_VMEM_BYTES - 6 * 1024 * 1024
_NEG_INF = float("-inf")
_LOG2_E = 1.4426950408889634
_QK_SCALE = _LOG2_E * _HEAD_DIM ** -0.5

_BF16 = jnp.bfloat16
_F32 = jnp.float32


def _params(vmem_bytes, semantics):
    limit = int(min(max(vmem_bytes, 32 * 1024 * 1024), _VMEM_CAP))
    return pltpu.CompilerParams(dimension_semantics=semantics, vmem_limit_bytes=limit)


def _dot(a, b):
    return jnp.dot(a, b, preferred_element_type=_F32)


def _dot_nt(a, b):
    return lax.dot_general(a, b, (((1,), (1,)), ((), ())), preferred_element_type=_F32)


def _pick_tile(n, candidates):
    for t in candidates:
        if n % t == 0:
            return t
    raise ValueError(f"no tile in {candidates} divides {n}")


def _cast_plan(weights, grid):
    n_steps = math.prod(grid)

    def step(*ids):
        s = ids[0]
        for n, i in zip(grid[1:], ids[1:]):
            s = s * n + i
        return s

    in_specs, out_specs, out_shapes, vmem = [], [], [], 0
    for w in weights:
        r, c = w.shape
        br = next(b for b in range(16, r + 1, 16) if r % b == 0 and r // b <= n_steps)
        last = r // br - 1

        def index(*ids, last=last):
            return jnp.minimum(step(*ids), last), 0

        in_specs.append(pl.BlockSpec((br, c), index))
        out_specs.append(pl.BlockSpec((br, c), index))
        out_shapes.append(jax.ShapeDtypeStruct((r, c), _BF16))
        vmem += 2 * br * c * (4 + 2) + br * c * 4
    return in_specs, out_specs, out_shapes, vmem


def _run_casts(srcs, dsts):
    for src, dst in zip(srcs, dsts):
        dst[...] = src[...].astype(dst.dtype)


def _mod_kernel(c_ref, w_ref, b_ref, o_ref):
    c = c_ref[...]
    s = (c * jax.nn.sigmoid(c)).astype(_BF16)
    o_ref[...] = _dot(s, w_ref[...].astype(_BF16)) + b_ref[...]


def _modulation(cc, ada_w, ada_b):
    rows, d = cc.shape
    n = ada_w.shape[1]
    tn = _pick_tile(n, (512, 256, 128))
    vmem = 2 * (d * tn * 4) + d * tn * 2 + 4 * rows * d * 4 + (4 << 20)
    return pl.pallas_call(
        _mod_kernel,
        grid=(n // tn,),
        in_specs=[
            pl.BlockSpec((rows, d), lambda j: (0, 0)),
            pl.BlockSpec((d, tn), lambda j: (0, j)),
            pl.BlockSpec((1, tn), lambda j: (0, j)),
        ],
        out_specs=pl.BlockSpec((rows, tn), lambda j: (0, j)),
        out_shape=jax.ShapeDtypeStruct((rows, n), _F32),
        compiler_params=_params(vmem, ("arbitrary",)),
        name="adaln_mod",
    )(cc, ada_w, ada_b)


def _norm_kernel(modulate, x_ref, g_ref, *rest):
    if modulate:
        sc_ref, sh_ref, o_ref = rest
    else:
        (o_ref,) = rest
    x = x_ref[...]
    ms = jnp.mean(x * x, axis=-1, keepdims=True)
    y = x * lax.rsqrt(ms + _EPS) * g_ref[...]
    if modulate:
        y = y * (1.0 + sc_ref[0]) + sh_ref[0]
    o_ref[...] = y.astype(o_ref.dtype)


def _rmsnorm(x, g, sc, sh, rows_per_group, out_dtype):
    m, d = x.shape
    tm = _pick_tile(rows_per_group, (512, 256, 128, 64, 32, 16, 8))
    tiles_per_group = rows_per_group // tm
    modulate = sc is not None
    in_specs = [pl.BlockSpec((tm, d), lambda i: (i, 0)), pl.BlockSpec((1, d), lambda i: (0, 0))]
    args = [x, g]
    if modulate:
        mod_spec = pl.BlockSpec((1, 1, d), lambda i: (i // tiles_per_group, 0, 0))
        in_specs += [mod_spec, mod_spec]
        args += [sc, sh]
    vmem = 2 * tm * d * 4 + 2 * tm * d * jnp.dtype(out_dtype).itemsize + 3 * tm * d * 4 + (4 << 20)
    return pl.pallas_call(
        functools.partial(_norm_kernel, modulate),
        grid=(m // tm,),
        in_specs=in_specs,
        out_specs=pl.BlockSpec((tm, d), lambda i: (i, 0)),
        out_shape=jax.ShapeDtypeStruct((m, d), out_dtype),
        compiler_params=_params(vmem, ("arbitrary",)),
        name="rmsnorm_mod" if modulate else "rmsnorm",
    )(*args)


def _rope_table_kernel(freq_ref, cos_ref, sin_ref):
    shape = cos_ref.shape
    t = lax.broadcasted_iota(jnp.int32, shape, 0)
    lane = lax.broadcasted_iota(jnp.int32, shape, 1)
    row = lax.shift_right_logical(t, _GRID_W.bit_length() - 1)
    col = jnp.bitwise_and(t, _GRID_W - 1)
    pos = jnp.where(lane < _HEAD_DIM // 2, row, col).astype(_F32)
    ang = pos * freq_ref[...]
    first_half = jnp.bitwise_and(lane, _HEAD_DIM // 2 - 1) < _HEAD_DIM // 4
    cos_ref[...] = jnp.cos(ang)
    sin_ref[...] = jnp.where(first_half, -jnp.sin(ang), jnp.sin(ang))


def _rope_tables(seq):
    r = _HEAD_DIM // 4
    freqs = _ROPE_THETA ** (-jnp.arange(r, dtype=_F32) / r)
    freq_lanes = jnp.tile(freqs, _HEAD_DIM // r)[None, :]
    out = jax.ShapeDtypeStruct((seq, _HEAD_DIM), _F32)
    return pl.pallas_call(
        _rope_table_kernel,
        out_shape=(out, out),
        name="rope_tables",
    )(freq_lanes)


def _inproj_tiles(kind, cm, h_ref, w_ref, gain_ref, cos_ref, sin_ref, o_ref):
    rows, tn = o_ref.shape
    w = w_ref[...]
    gain = gain_ref[0]
    if kind == "normrope":
        lane = lax.broadcasted_iota(jnp.int32, (cm, _HEAD_DIM), 1)
        first_half = jnp.bitwise_and(lane, _HEAD_DIM // 2 - 1) < _HEAD_DIM // 4
    for c in range(rows // cm):
        rs = slice(c * cm, (c + 1) * cm)
        acc = _dot(h_ref[rs, :], w)
        if kind == "plain":
            o_ref[rs, :] = acc.astype(o_ref.dtype)
        elif kind == "scaled":
            o_ref[rs, :] = (acc * gain[:, :1]).astype(o_ref.dtype)
        elif kind == "sigmoid":
            o_ref[rs, :] = jax.nn.sigmoid(acc).astype(o_ref.dtype)
        else:
            cos = cos_ref[rs, :]
            sin = sin_ref[rs, :]
            for hc in range(tn // _HEAD_DIM):
                cs = slice(hc * _HEAD_DIM, (hc + 1) * _HEAD_DIM)
                xh = acc[:, cs]
                ms = jnp.mean(xh * xh, axis=-1, keepdims=True)
                y = xh * lax.rsqrt(ms + _EPS) * gain
                partner = jnp.where(
                    first_half,
                    pltpu.roll(y, _HEAD_DIM - _HEAD_DIM // 4, axis=1),
                    pltpu.roll(y, _HEAD_DIM // 4, axis=1),
                )
                o_ref[rs, cs] = (y * cos + partner * sin).astype(o_ref.dtype)


def _inproj_kernel(tile_kinds, cm, *refs):
    if len(tile_kinds) == 1:
        _inproj_tiles(tile_kinds[0][2], cm, *refs)
        return
    j = pl.program_id(1)
    for lo, hi, kind in tile_kinds:
        pl.when((j >= lo) & (j < hi))(functools.partial(_inproj_tiles, kind, cm, *refs))


def _inproj(h, w, segments, slab_rows, cos, sin):
    m, d = h.shape
    n = sum(width for _, width, _, _ in segments)
    edges = [v for start, width, _, _ in segments for v in (start, width)]
    tn = _pick_tile(functools.reduce(math.gcd, edges), (512, 256, 128))
    cm = _pick_tile(slab_rows, (512, 256, 128, 64, 32, 16))

    tile_kinds, gains, first = [], [], 0
    for start, width, kind, gain in segments:
        tiles = width // tn
        if tile_kinds and tile_kinds[-1][2] == kind:
            tile_kinds[-1] = (tile_kinds[-1][0], first + tiles, kind)
        else:
            tile_kinds.append((first, first + tiles, kind))
        gain = jnp.ones((_HEAD_DIM,), _F32) if gain is None else gain
        gains.append(jnp.broadcast_to(gain.astype(_F32), (tiles, 1, _HEAD_DIM)))
        first += tiles

    def w_block(j):
        blk, first = None, 0
        for start, width, _, _ in segments:
            here = start // tn + (j - first)
            blk = here if blk is None else jnp.where(j >= first, here, blk)
            first += width // tn
        return blk

    vmem = (slab_rows * d * 2 + 2 * d * tn * 2 + 2 * slab_rows * tn * 2 + 6 * cm * tn * 4
            + 2 * slab_rows * _HEAD_DIM * 4 + (6 << 20))
    return pl.pallas_call(
        functools.partial(_inproj_kernel, tile_kinds, cm),
        grid=(m // slab_rows, n // tn),
        in_specs=[
            pl.BlockSpec((slab_rows, d), lambda b, j: (b, 0), pipeline_mode=pl.Buffered(1)),
            pl.BlockSpec((d, tn), lambda b, j: (0, w_block(j))),
            pl.BlockSpec((1, 1, _HEAD_DIM), lambda b, j: (j, 0, 0)),
            pl.BlockSpec((slab_rows, _HEAD_DIM), lambda b, j: (0, 0), pipeline_mode=pl.Buffered(1)),
            pl.BlockSpec((slab_rows, _HEAD_DIM), lambda b, j: (0, 0), pipeline_mode=pl.Buffered(1)),
        ],
        out_specs=pl.BlockSpec((slab_rows, tn), lambda b, j: (b, j)),
        out_shape=jax.ShapeDtypeStruct((m, n), _BF16),
        compiler_params=_params(vmem, ("arbitrary", "arbitrary")),
        name="inproj",
    )(h, w, jnp.concatenate(gains, axis=0), cos, sin)


def _softmax_pv(s_ref, p_ref, n_ctx, vc, v):
    s = s_ref[...]
    m = jnp.max(s, axis=-1, keepdims=True)
    p = jnp.exp2(s - m)
    denom = jnp.sum(p, axis=-1, keepdims=True)
    p_ref[...] = p.astype(p_ref.dtype)
    o = _dot(p_ref[:, :n_ctx], vc) + _dot(p_ref[:, n_ctx:], v)
    return o / denom


def _attn_a_kernel(n_cast, q_ref, k_ref, v_ref, kc_ref, vc_ref, *rest):
    cast_srcs, o_ref, cast_dsts = rest[:n_cast], rest[n_cast], rest[n_cast + 1:2 * n_cast + 1]
    s_ref, p_ref = rest[2 * n_cast + 1:]
    _run_casts(cast_srcs, cast_dsts)
    n_ctx = kc_ref.shape[0]
    k = k_ref[...]
    v = v_ref[...]
    kc = kc_ref[...]
    vc = vc_ref[...]

    ts = s_ref.shape[1]
    stages = [(slice(t * ts, (t + 1) * ts), slice(g * _HEAD_DIM, (g + 1) * _HEAD_DIM))
              for t in range(q_ref.shape[0] // ts) for g in range(_GROUPS_A)]

    def scores(n):
        q = q_ref[stages[n]]
        s_ref[n % 2, :, :n_ctx] = _dot_nt(q, kc)
        s_ref[n % 2, :, n_ctx:] = _dot_nt(q, k)

    def finish(n):
        o = _softmax_pv(s_ref.at[n % 2], p_ref.at[n % 2], n_ctx, vc, v)
        o_ref[stages[n]] = o.astype(o_ref.dtype)

    scores(0)
    for n in range(len(stages)):
        if n + 1 < len(stages):
            scores(n + 1)
        finish(n)


def _attn_a(p, pc, batch, seq, ctx_len, kv_heads, k_off, v_off, kc_off, vc_off, casts):
    wq = kv_heads * _GROUPS_A * _HEAD_DIM
    gw = _GROUPS_A * _HEAD_DIM
    tq = _pick_tile(seq, (512, 256, 128))
    ts = min(tq, 256)
    nq = seq // tq
    n_keys = seq + ctx_len
    vmem = (2 * tq * gw * 2 * 2 + 4 * seq * _HEAD_DIM * 2 + 4 * ctx_len * _HEAD_DIM * 2
            + 2 * ts * n_keys * 6 + 3 * ts * n_keys * 4 + (6 << 20))
    grid = (batch, kv_heads, nq)
    cast_in, cast_out, cast_shapes, cast_vmem = _cast_plan(casts, grid)
    return pl.pallas_call(
        functools.partial(_attn_a_kernel, len(casts)),
        grid=grid,
        in_specs=[
            pl.BlockSpec((tq, gw), lambda b, h, i: (b * nq + i, h)),
            pl.BlockSpec((seq, _HEAD_DIM), lambda b, h, i: (b, k_off + h)),
            pl.BlockSpec((seq, _HEAD_DIM), lambda b, h, i: (b, v_off + h)),
            pl.BlockSpec((ctx_len, _HEAD_DIM), lambda b, h, i: (b, kc_off + h)),
            pl.BlockSpec((ctx_len, _HEAD_DIM), lambda b, h, i: (b, vc_off + h)),
        ] + cast_in,
        out_specs=[pl.BlockSpec((tq, gw), lambda b, h, i: (b * nq + i, h))] + cast_out,
        out_shape=[jax.ShapeDtypeStruct((batch * seq, wq), _BF16)] + cast_shapes,
        scratch_shapes=[pltpu.VMEM((2, ts, n_keys), _F32), pltpu.VMEM((2, ts, n_keys), _BF16)],
        compiler_params=_params(vmem + cast_vmem, ("arbitrary", "arbitrary", "arbitrary")),
        name="attn_gqa",
    )(p, p, p, pc, pc, *casts)


def _window_start_row(blk, grid_rows):
    lo = _BLK_ROWS * blk - _WIN_H // 2
    if isinstance(blk, int):
        return min(max(lo, 0), grid_rows - _KEY_ROWS)
    return jnp.clip(lo, 0, grid_rows - _KEY_ROWS)


def _bias_table_kernel(grid_rows, rpb_ref, o_ref):
    h = pl.program_id(0)
    n_dr = 2 * _WIN_H - 1
    n_dc = 2 * _WIN_W - 1
    shape = (_GRID_W, 2 * _GRID_W)
    qc = lax.broadcasted_iota(jnp.int32, shape, 0)
    lane = lax.broadcasted_iota(jnp.int32, shape, 1)
    col_start = jnp.clip(qc - _WIN_W // 2, 0, _GRID_W - _WIN_W)
    neg = jnp.full(shape, _NEG_INF, _F32)

    def side(kc, on_side):
        ok = on_side & (kc >= col_start) & (kc < col_start + _WIN_W)
        return kc - qc + (_WIN_W - 1), ok

    diff_l, ok_l = side(lane, lane < _GRID_W)
    diff_r, ok_r = side(lane - _GRID_W, lane >= _GRID_W)
    left, right = [], []
    for dr in range(n_dr):
        acc_l, acc_r = neg, neg
        for dc in range(n_dc):
            val = rpb_ref[(h * n_dr + dr) * n_dc + dc] * _LOG2_E
            acc_l = jnp.where(ok_l & (diff_l == dc), val, acc_l)
            acc_r = jnp.where(ok_r & (diff_r == dc), val, acc_r)
        left.append(acc_l)
        right.append(acc_r)

    n_blk = grid_rows // _BLK_ROWS
    for variant, blk in enumerate((0, 1, n_blk - 1)):
        ws = _window_start_row(blk, grid_rows)
        for a in range(_BLK_ROWS):
            qr = _BLK_ROWS * blk + a
            rs = min(max(qr - _WIN_H // 2, 0), grid_rows - _WIN_H)
            for jp in range(_KEY_ROWS // 2):
                piece = None
                for half, table in ((0, left), (1, right)):
                    kr = ws + 2 * jp + half
                    if rs <= kr < rs + _WIN_H:
                        blk_bias = table[kr - qr + _WIN_H - 1]
                        piece = blk_bias if piece is None else jnp.maximum(piece, blk_bias)
                if piece is None:
                    piece = neg
                o_ref[0, variant, a * _GRID_W:(a + 1) * _GRID_W,
                      jp * 2 * _GRID_W:(jp + 1) * 2 * _GRID_W] = piece


def _bias_tables(rpb, grid_rows):
    heads = rpb.shape[0]
    blk = (1, 3, _BLK_ROWS * _GRID_W, _KEY_ROWS * _GRID_W)
    return pl.pallas_call(
        functools.partial(_bias_table_kernel, grid_rows),
        grid=(heads,),
        in_specs=[pl.BlockSpec(memory_space=pltpu.SMEM)],
        out_specs=pl.BlockSpec(blk, lambda h: (h, 0, 0, 0)),
        out_shape=jax.ShapeDtypeStruct((heads,) + blk[1:], _F32),
        compiler_params=_params(3 * blk[1] * blk[2] * blk[3] * 4 + (8 << 20), ("arbitrary",)),
        name="nbr_bias_tables",
    )(rpb.reshape(-1))


def _attn_b_kernel(grid_rows, n_cast, q_ref, k_ref, v_ref, kc_ref, vc_ref, bias_ref, *rest):
    cast_srcs, o_ref, cast_dsts = rest[:n_cast], rest[n_cast], rest[n_cast + 1:2 * n_cast + 1]
    s_ref, p_ref = rest[2 * n_cast + 1:]
    _run_casts(cast_srcs, cast_dsts)
    n_ctx = kc_ref.shape[0]
    n_blk = grid_rows // _BLK_ROWS
    tq = _BLK_ROWS * _GRID_W
    span = _KEY_ROWS * _GRID_W
    kc = kc_ref[...]
    vc = vc_ref[...]

    def window(i):
        start = _window_start_row(i, grid_rows) * _GRID_W
        return slice(start, start + span)

    def scores(i):
        q = q_ref[i * tq:(i + 1) * tq, :]
        variant = 0 if i == 0 else (2 if i == n_blk - 1 else 1)
        s_ref[i % 2, :, :n_ctx] = _dot_nt(q, kc)
        s_ref[i % 2, :, n_ctx:] = _dot_nt(q, k_ref[window(i), :]) + bias_ref[0, variant]

    def finish(i):
        o = _softmax_pv(s_ref.at[i % 2], p_ref.at[i % 2], n_ctx, vc, v_ref[window(i), :])
        o_ref[i * tq:(i + 1) * tq, :] = o.astype(o_ref.dtype)

    scores(0)
    for i in range(n_blk):
        if i + 1 < n_blk:
            scores(i + 1)
        finish(i)


def _attn_b(p, pc, bias, batch, seq, ctx_len, heads, q_off, k_off, v_off, kc_off, vc_off, casts):
    grid_rows = seq // _GRID_W
    tq = _BLK_ROWS * _GRID_W
    span = _KEY_ROWS * _GRID_W
    n_keys = span + ctx_len
    vmem = (4 * 2 * seq * _HEAD_DIM * 2 + 4 * ctx_len * _HEAD_DIM * 2
            + 2 * 3 * tq * span * 4 + 2 * tq * n_keys * 6 + 3 * tq * n_keys * 4 + (6 << 20))
    grid = (heads, batch)
    cast_in, cast_out, cast_shapes, cast_vmem = _cast_plan(casts, grid)
    return pl.pallas_call(
        functools.partial(_attn_b_kernel, grid_rows, len(casts)),
        grid=grid,
        in_specs=[
            pl.BlockSpec((seq, _HEAD_DIM), lambda h, b: (b, q_off + h)),
            pl.BlockSpec((seq, _HEAD_DIM), lambda h, b: (b, k_off + h)),
            pl.BlockSpec((seq, _HEAD_DIM), lambda h, b: (b, v_off + h)),
            pl.BlockSpec((ctx_len, _HEAD_DIM), lambda h, b: (b, kc_off + h)),
            pl.BlockSpec((ctx_len, _HEAD_DIM), lambda h, b: (b, vc_off + h)),
            pl.BlockSpec((1, 3, tq, span), lambda h, b: (h, 0, 0, 0)),
        ] + cast_in,
        out_specs=[pl.BlockSpec((seq, _HEAD_DIM), lambda h, b: (b, h))] + cast_out,
        out_shape=[jax.ShapeDtypeStruct((batch * seq, heads * _HEAD_DIM), _BF16)] + cast_shapes,
        scratch_shapes=[pltpu.VMEM((2, tq, n_keys), _F32), pltpu.VMEM((2, tq, n_keys), _BF16)],
        compiler_params=_params(vmem + cast_vmem, ("arbitrary", "arbitrary")),
        name="attn_nbr",
    )(p, p, p, pc, pc, bias, *casts)


def _merge_kernel(oa_ref, ob_ref, wa_ref, wb_ref, ga_ref, gb_ref, y_ref):
    ya = _dot(oa_ref[...], wa_ref[...])
    yb = _dot(ob_ref[...], wb_ref[...])
    y = ga_ref[...].astype(_F32) * ya + gb_ref[...].astype(_F32) * yb
    y_ref[...] = y.astype(y_ref.dtype)


def _merge(o_a, o_b, w_a, w_b, p, gate_col, d_model):
    m, ka = o_a.shape
    kb = o_b.shape[1]
    tm = _pick_tile(m, (1024, 512, 256, 128))
    tn = _pick_tile(math.gcd(d_model, gate_col), (1024, 512, 256, 128))
    nj = d_model // tn
    g0 = gate_col // tn
    vmem = 2 * (tm * (ka + kb) * 2 + (ka + kb) * tn * 2 + 3 * tm * tn * 2) + 4 * tm * tn * 4 + (6 << 20)
    return pl.pallas_call(
        _merge_kernel,
        grid=(m // tm, nj),
        in_specs=[
            pl.BlockSpec((tm, ka), lambda i, j: (i, 0)),
            pl.BlockSpec((tm, kb), lambda i, j: (i, 0)),
            pl.BlockSpec((ka, tn), lambda i, j: (0, j)),
            pl.BlockSpec((kb, tn), lambda i, j: (0, j)),
            pl.BlockSpec((tm, tn), lambda i, j: (i, g0 + j)),
            pl.BlockSpec((tm, tn), lambda i, j: (i, g0 + nj + j)),
        ],
        out_specs=pl.BlockSpec((tm, tn), lambda i, j: (i, j)),
        out_shape=jax.ShapeDtypeStruct((m, d_model), _BF16),
        compiler_params=_params(vmem, ("arbitrary", "arbitrary")),
        name="branch_merge",
    )(o_a, o_b, w_a, w_b, p, p)


def _mm_residual_kernel(a_ref, w_ref, res_ref, gate_ref, o_ref):
    acc = _dot(a_ref[...], w_ref[...])
    o_ref[...] = res_ref[...] + gate_ref[0] * acc


def _mm_residual(a, w, res, gate, rows_per_group, tm_cands, tn_cands, name):
    m, k = a.shape
    n = w.shape[1]
    tm = _pick_tile(rows_per_group, tm_cands)
    tn = _pick_tile(n, tn_cands)
    tiles_per_group = rows_per_group // tm
    vmem = 2 * (tm * k * 2 + k * tn * 2 + 2 * tm * tn * 4) + 2 * tm * tn * 4 + (6 << 20)
    return pl.pallas_call(
        _mm_residual_kernel,
        grid=(m // tm, n // tn),
        in_specs=[
            pl.BlockSpec((tm, k), lambda i, j: (i, 0)),
            pl.BlockSpec((k, tn), lambda i, j: (0, j)),
            pl.BlockSpec((tm, tn), lambda i, j: (i, j)),
            pl.BlockSpec((1, 1, tn), lambda i, j: (i // tiles_per_group, 0, j)),
        ],
        out_specs=pl.BlockSpec((tm, tn), lambda i, j: (i, j)),
        out_shape=jax.ShapeDtypeStruct((m, n), _F32),
        compiler_params=_params(vmem, ("arbitrary", "arbitrary")),
        name=name,
    )(a, w, res, gate)


def _ffn_up_kernel(cm, h_ref, wg_ref, wv_ref, cwg_ref, cwv_ref, cbg_ref, cbv_ref, o_ref):
    rows, tn = o_ref.shape
    n_chunks = rows // cm
    wg = wg_ref[...]
    wv = wv_ref[...]
    ug, uv = [], []
    for c in range(n_chunks):
        hs = h_ref[c * cm:(c + 1) * cm, :]
        ug.append(_dot(hs, wg))
        uv.append(_dot(hs, wv))
    groups = cm // _SUBLANES
    sub = lax.broadcasted_iota(jnp.int32, (groups, _SUBLANES, tn), 1)
    zero_group = jnp.zeros((1, _SUBLANES, tn), _F32)

    def rotations(u):
        u3 = [x.reshape(groups, _SUBLANES, tn) for x in u]
        down = [pltpu.roll(x, 1, axis=1) for x in u3]
        up = [pltpu.roll(x, _SUBLANES - 1, axis=1) for x in u3]
        return u3, down, up

    def conv(rot, c, cw_ref, cb_ref):
        u3, down, up = rot
        before = down[c - 1][groups - 1:] if c > 0 else zero_group
        after = up[c + 1][:1] if c + 1 < n_chunks else zero_group
        prev = jnp.where(sub == 0, jnp.concatenate([before, down[c][:groups - 1]], axis=0), down[c])
        nxt = jnp.where(sub == _SUBLANES - 1, jnp.concatenate([up[c][1:], after], axis=0), up[c])
        out = prev * cw_ref[0:1, :] + u3[c] * cw_ref[1:2, :] + nxt * cw_ref[2:3, :] + cb_ref[...]
        return out.reshape(cm, tn)

    rot_g = rotations(ug)
    rot_v = rotations(uv)
    for c in range(n_chunks):
        gate = conv(rot_g, c, cwg_ref, cbg_ref)
        val = conv(rot_v, c, cwv_ref, cbv_ref)
        o_ref[c * cm:(c + 1) * cm, :] = (gate * jax.nn.sigmoid(gate) * val).astype(o_ref.dtype)


def _ffn_up(h2, w_up, conv_w, conv_b, seq):
    m, d = h2.shape
    d_ff = w_up.shape[1] // 2
    tn = _pick_tile(d_ff, (256, 128))
    nj = d_ff // tn
    cm = _pick_tile(seq, (512, 256, 128, 64, 32, 16))
    taps = conv_w.shape[0]
    vmem = seq * d * 2 + 2 * 2 * d * tn * 2 + 2 * seq * tn * 2 + 6 * seq * tn * 4 + (6 << 20)
    return pl.pallas_call(
        functools.partial(_ffn_up_kernel, cm),
        grid=(m // seq, nj),
        in_specs=[
            pl.BlockSpec((seq, d), lambda b, j: (b, 0), pipeline_mode=pl.Buffered(1)),
            pl.BlockSpec((d, tn), lambda b, j: (0, j)),
            pl.BlockSpec((d, tn), lambda b, j: (0, nj + j)),
            pl.BlockSpec((taps, tn), lambda b, j: (0, j)),
            pl.BlockSpec((taps, tn), lambda b, j: (0, nj + j)),
            pl.BlockSpec((1, tn), lambda b, j: (0, j)),
            pl.BlockSpec((1, tn), lambda b, j: (0, nj + j)),
        ],
        out_specs=pl.BlockSpec((seq, tn), lambda b, j: (b, j)),
        out_shape=jax.ShapeDtypeStruct((m, d_ff), _BF16),
        compiler_params=_params(vmem, ("arbitrary", "arbitrary")),
        name="ffn_up_conv_gate",
    )(h2, w_up, w_up, conv_w, conv_w, conv_b, conv_b)


def kernel(x, c, ctx, c_ctx, ada_w, ada_b, norm1_g, w_in, qnorm_a, knorm_a, rpb_b,
           w_br_a, w_br_b, w_out, norm2_g, w_up, conv_w, conv_b, w_down, final_g):
    batch, seq, d = x.shape
    ctx_len = ctx.shape[1]
    depth = ada_w.shape[0]
    wa_q = w_br_a.shape[1]
    wb = w_br_b.shape[1]
    wa_kv = (w_in.shape[2] - wa_q - 3 * wb - 2 * d) // 2
    kv_heads = wa_kv // _HEAD_DIM
    heads_b = wb // _HEAD_DIM
    assert depth == 1, "the context-stream update between layers is not implemented"
    assert wa_q == kv_heads * _GROUPS_A * _HEAD_DIM and seq % (_BLK_ROWS * _GRID_W) == 0
    assert seq // _GRID_W >= _KEY_ROWS and qnorm_a.shape[-1] == _HEAD_DIM
    grid_rows = seq // _GRID_W
    m = batch * seq
    xf = x.reshape(m, d)
    ctxf = ctx.reshape(batch * ctx_len, d)

    pad_rows = -(batch + 1) % 16
    cc = jnp.concatenate([c, c_ctx[None, :], jnp.zeros((pad_rows, d), _F32)], axis=0)
    mod = _modulation(cc, ada_w[0], ada_b[0][None, :])
    sh1, sc1, g1, sh2, sc2, g2 = [mod[:batch, i * d:(i + 1) * d][:, None, :] for i in range(6)]
    csh1, csc1 = [mod[batch:batch + 1, i * d:(i + 1) * d][:, None, :] for i in range(2)]

    w_in16 = w_in[0].astype(_BF16)
    n_in = w_in16.shape[1]
    o_k = wa_q
    o_v = o_k + wa_kv
    o_qb = o_v + wa_kv
    o_kb = o_qb + wb
    o_g = o_qb + 3 * wb

    g1n = norm1_g[0][None, :]
    h = _rmsnorm(xf, g1n, sc1, sh1, seq, _BF16)
    hc = _rmsnorm(ctxf, g1n, csc1, csh1, batch * ctx_len, _BF16)

    cos_t, sin_t = _rope_tables(seq)
    q_scale = jnp.full((_HEAD_DIM,), _QK_SCALE, _F32)
    p = _inproj(h, w_in16, [
        (0, wa_q, "normrope", qnorm_a[0] * _QK_SCALE),
        (o_k, wa_kv, "normrope", knorm_a[0]),
        (o_v, wa_kv, "plain", None),
        (o_qb, wb, "scaled", q_scale),
        (o_kb, 2 * wb, "plain", None),
        (o_g, n_in - o_g, "sigmoid", None),
    ], seq, cos_t, sin_t)
    c_rows = batch * ctx_len
    pc = _inproj(hc, w_in16, [
        (o_k, wa_kv, "normrope", knorm_a[0]),
        (o_v, wa_kv, "plain", None),
        (o_kb, 2 * wb, "plain", None),
    ], c_rows, jnp.ones((c_rows, _HEAD_DIM), _F32), jnp.zeros((c_rows, _HEAD_DIM), _F32))
    blk = lambda col: col // _HEAD_DIM
    pc_v, pc_kb, pc_vb = wa_kv, 2 * wa_kv, 2 * wa_kv + wb

    o_a, w_up16, w_down16 = _attn_a(
        p, pc, batch, seq, ctx_len, kv_heads, k_off=blk(o_k), v_off=blk(o_v), kc_off=0, vc_off=blk(pc_v),
        casts=[w_up[0], w_down[0]])
    bias = _bias_tables(rpb_b[0], grid_rows)
    o_b, w_bra16, w_brb16, w_out16 = _attn_b(
        p, pc, bias, batch, seq, ctx_len, heads_b,
        q_off=blk(o_qb), k_off=blk(o_kb), v_off=blk(o_kb + wb), kc_off=blk(pc_kb), vc_off=blk(pc_vb),
        casts=[w_br_a[0], w_br_b[0], w_out[0]])

    y = _merge(o_a, o_b, w_bra16, w_brb16, p, o_g, d)
    x1 = _mm_residual(y, w_out16, xf, g1, seq,
                      (1024, 512, 256, 128), (1024, 512, 256, 128), "out_proj_residual")

    h2 = _rmsnorm(x1, norm2_g[0][None, :], sc2, sh2, seq, _BF16)
    t = _ffn_up(h2, w_up16, conv_w[0], conv_b[0][None, :], seq)
    x2 = _mm_residual(t, w_down16, x1, g2, seq,
                      (512, 256, 128), (512, 256, 128), "ffn_down_residual")

    out = _rmsnorm(x2, final_g[None, :], None, None, seq, _F32)
    return out.reshape(batch, seq, d)
```

```python
import functools
import math

import jax
import jax.numpy as jnp
from jax import lax
from jax.experimental import pallas as pl
from jax.experimental.pallas import tpu as pltpu

_HEAD_DIM = 128
_GRID_W = 64
_WIN_H = 8
_WIN_W = 16
_GROUPS_A = 4
_ROPE_THETA = 10000.0
_EPS = 1e-6
_BLK_ROWS = 4
_KEY_ROWS = _BLK_ROWS + _WIN_H
_SUBLANES = 8
_V7X_VMEM_BYTES = 64 * 1024 * 1024
_VMEM_CAP = _V7X_VMEM_BYTES - 6 * 1024 * 1024
_NEG_INF = float("-inf")
_LOG2_E = 1.4426950408889634
_QK_SCALE = _LOG2_E * _HEAD_DIM ** -0.5

_BF16 = jnp.bfloat16
_F32 = jnp.float32


def _params(vmem_bytes, semantics):
    limit = int(min(max(vmem_bytes, 32 * 1024 * 1024), _VMEM_CAP))
    return pltpu.CompilerParams(dimension_semantics=semantics, vmem_limit_bytes=limit)


def _dot(a, b):
    return jnp.dot(a, b, preferred_element_type=_F32)


def _dot_nt(a, b):
    return lax.dot_general(a, b, (((1,), (1,)), ((), ())), preferred_element_type=_F32)


def _pick_tile(n, candidates):
    for t in candidates:
        if n % t == 0:
            return t
    raise ValueError(f"no tile in {candidates} divides {n}")


def _cast_plan(weights, grid):
    n_steps = math.prod(grid)

    def step(*ids):
        s = ids[0]
        for n, i in zip(grid[1:], ids[1:]):
            s = s * n + i
        return s

    in_specs, out_specs, out_shapes, vmem = [], [], [], 0
    for w in weights:
        r, c = w.shape
        br = next(b for b in range(16, r + 1, 16) if r % b == 0 and r // b <= n_steps)
        last = r // br - 1

        def index(*ids, last=last):
            return jnp.minimum(step(*ids), last), 0

        in_specs.append(pl.BlockSpec((br, c), index))
        out_specs.append(pl.BlockSpec((br, c), index))
        out_shapes.append(jax.ShapeDtypeStruct((r, c), _BF16))
        vmem += 2 * br * c * (4 + 2) + br * c * 4
    return in_specs, out_specs, out_shapes, vmem


def _run_casts(srcs, dsts):
    for src, dst in zip(srcs, dsts):
        dst[...] = src[...].astype(dst.dtype)


def _mod_kernel(c_ref, w_ref, b_ref, o_ref):
    c = c_ref[...]
    s = (c * jax.nn.sigmoid(c)).astype(_BF16)
    o_ref[...] = _dot(s, w_ref[...].astype(_BF16)) + b_ref[...]


def _modulation(cc, ada_w, ada_b):
    rows, d = cc.shape
    n = ada_w.shape[1]
    tn = _pick_tile(n, (512, 256, 128))
    vmem = 2 * (d * tn * 4) + d * tn * 2 + 4 * rows * d * 4 + (4 << 20)
    return pl.pallas_call(
        _mod_kernel,
        grid=(n // tn,),
        in_specs=[
            pl.BlockSpec((rows, d), lambda j: (0, 0)),
            pl.BlockSpec((d, tn), lambda j: (0, j)),
            pl.BlockSpec((1, tn), lambda j: (0, j)),
        ],
        out_specs=pl.BlockSpec((rows, tn), lambda j: (0, j)),
        out_shape=jax.ShapeDtypeStruct((rows, n), _F32),
        compiler_params=_params(vmem, ("arbitrary",)),
        name="adaln_mod",
    )(cc, ada_w, ada_b)


def _norm_kernel(modulate, x_ref, g_ref, *rest):
    if modulate:
        sc_ref, sh_ref, o_ref = rest
    else:
        (o_ref,) = rest
    x = x_ref[...]
    ms = jnp.mean(x * x, axis=-1, keepdims=True)
    y = x * lax.rsqrt(ms + _EPS) * g_ref[...]
    if modulate:
        y = y * (1.0 + sc_ref[0]) + sh_ref[0]
    o_ref[...] = y.astype(o_ref.dtype)


def _rmsnorm(x, g, sc, sh, rows_per_group, out_dtype):
    m, d = x.shape
    tm = _pick_tile(rows_per_group, (512, 256, 128, 64, 32, 16, 8))
    tiles_per_group = rows_per_group // tm
    modulate = sc is not None
    in_specs = [pl.BlockSpec((tm, d), lambda i: (i, 0)), pl.BlockSpec((1, d), lambda i: (0, 0))]
    args = [x, g]
    if modulate:
        mod_spec = pl.BlockSpec((1, 1, d), lambda i: (i // tiles_per_group, 0, 0))
        in_specs += [mod_spec, mod_spec]
        args += [sc, sh]
    vmem = 2 * tm * d * 4 + 2 * tm * d * jnp.dtype(out_dtype).itemsize + 3 * tm * d * 4 + (4 << 20)
    return pl.pallas_call(
        functools.partial(_norm_kernel, modulate),
        grid=(m // tm,),
        in_specs=in_specs,
        out_specs=pl.BlockSpec((tm, d), lambda i: (i, 0)),
        out_shape=jax.ShapeDtypeStruct((m, d), out_dtype),
        compiler_params=_params(vmem, ("arbitrary",)),
        name="rmsnorm_mod" if modulate else "rmsnorm",
    )(*args)


def _rope_table_kernel(freq_ref, cos_ref, sin_ref):
    shape = cos_ref.shape
    t = lax.broadcasted_iota(jnp.int32, shape, 0)
    lane = lax.broadcasted_iota(jnp.int32, shape, 1)
    row = lax.shift_right_logical(t, _GRID_W.bit_length() - 1)
    col = jnp.bitwise_and(t, _GRID_W - 1)
    pos = jnp.where(lane < _HEAD_DIM // 2, row, col).astype(_F32)
    ang = pos * freq_ref[...]
    first_half = jnp.bitwise_and(lane, _HEAD_DIM // 2 - 1) < _HEAD_DIM // 4
    cos_ref[...] = jnp.cos(ang)
    sin_ref[...] = jnp.where(first_half, -jnp.sin(ang), jnp.sin(ang))


def _rope_tables(seq):
    r = _HEAD_DIM // 4
    freqs = _ROPE_THETA ** (-jnp.arange(r, dtype=_F32) / r)
    freq_lanes = jnp.tile(freqs, _HEAD_DIM // r)[None, :]
    out = jax.ShapeDtypeStruct((seq, _HEAD_DIM), _F32)
    return pl.pallas_call(
        _rope_table_kernel,
        out_shape=(out, out),
        name="rope_tables",
    )(freq_lanes)


def _inproj_kernel(kind, cm, h_ref, w_ref, *rest):
    if kind == "normrope":
        gain_ref, cos_ref, sin_ref, o_ref = rest
    elif kind == "scaled":
        gain_ref, o_ref = rest
    else:
        (o_ref,) = rest
    rows, tn = o_ref.shape
    w = w_ref[...]
    if kind in ("normrope", "scaled"):
        gain = gain_ref[0]
    if kind == "normrope":
        lane = lax.broadcasted_iota(jnp.int32, (cm, _HEAD_DIM), 1)
        first_half = jnp.bitwise_and(lane, _HEAD_DIM // 2 - 1) < _HEAD_DIM // 4
    for c in range(rows // cm):
        rs = slice(c * cm, (c + 1) * cm)
        acc = _dot(h_ref[rs, :], w)
        if kind == "plain":
            o_ref[rs, :] = acc.astype(o_ref.dtype)
        elif kind == "scaled":
            o_ref[rs, :] = (acc * gain[:, :1]).astype(o_ref.dtype)
        elif kind == "sigmoid":
            o_ref[rs, :] = jax.nn.sigmoid(acc).astype(o_ref.dtype)
        else:
            cos = cos_ref[rs, :]
            sin = sin_ref[rs, :]
            for hc in range(tn // _HEAD_DIM):
                cs = slice(hc * _HEAD_DIM, (hc + 1) * _HEAD_DIM)
                xh = acc[:, cs]
                ms = jnp.mean(xh * xh, axis=-1, keepdims=True)
                y = xh * lax.rsqrt(ms + _EPS) * gain
                partner = jnp.where(
                    first_half,
                    pltpu.roll(y, _HEAD_DIM - _HEAD_DIM // 4, axis=1),
                    pltpu.roll(y, _HEAD_DIM // 4, axis=1),
                )
                o_ref[rs, cs] = (y * cos + partner * sin).astype(o_ref.dtype)


def _inproj_tile(col_ranges):
    edges = [v for start_width in col_ranges for v in start_width]
    return _pick_tile(functools.reduce(math.gcd, edges), (512, 256, 128))


def _inproj(h, w, col_ranges, kind, slab_rows, gains=None, cos=None, sin=None):
    m, d = h.shape
    n = sum(width for _, width in col_ranges)
    tn = _inproj_tile(col_ranges)
    cm = _pick_tile(slab_rows, (512, 256, 128, 64, 32, 16))

    def w_block(j):
        blk, first = None, 0
        for start, width in col_ranges:
            here = start // tn + (j - first)
            blk = here if blk is None else jnp.where(j >= first, here, blk)
            first += width // tn
        return blk

    in_specs = [
        pl.BlockSpec((slab_rows, d), lambda b, j: (b, 0), pipeline_mode=pl.Buffered(1)),
        pl.BlockSpec((d, tn), lambda b, j: (0, w_block(j))),
    ]
    args = [h, w]
    vmem = slab_rows * d * 2 + 2 * d * tn * 2 + 2 * slab_rows * tn * 2 + 6 * cm * tn * 4 + (6 << 20)
    if kind in ("normrope", "scaled"):
        in_specs.append(pl.BlockSpec((1, 1, _HEAD_DIM), lambda b, j: (j, 0, 0)))
        args.append(gains)
    if kind == "normrope":
        in_specs += [
            pl.BlockSpec((slab_rows, _HEAD_DIM), lambda b, j: (0, 0), pipeline_mode=pl.Buffered(1)),
            pl.BlockSpec((slab_rows, _HEAD_DIM), lambda b, j: (0, 0), pipeline_mode=pl.Buffered(1)),
        ]
        args += [cos, sin]
        vmem += 2 * slab_rows * _HEAD_DIM * 4
    return pl.pallas_call(
        functools.partial(_inproj_kernel, kind, cm),
        grid=(m // slab_rows, n // tn),
        in_specs=in_specs,
        out_specs=pl.BlockSpec((slab_rows, tn), lambda b, j: (b, j)),
        out_shape=jax.ShapeDtypeStruct((m, n), _BF16),
        compiler_params=_params(vmem, ("arbitrary", "arbitrary")),
        name="inproj_" + kind,
    )(*args)


def _softmax_pv(s_ref, p_ref, n_ctx, vc, v):
    s = s_ref[...]
    m = jnp.max(s, axis=-1, keepdims=True)
    p = jnp.exp2(s - m)
    denom = jnp.sum(p, axis=-1, keepdims=True)
    p_ref[...] = p.astype(p_ref.dtype)
    o = _dot(p_ref[:, :n_ctx], vc) + _dot(p_ref[:, n_ctx:], v)
    return o / denom


def _attn_a_kernel(n_cast, q_ref, k_ref, v_ref, kc_ref, vc_ref, *rest):
    cast_srcs, o_ref, cast_dsts = rest[:n_cast], rest[n_cast], rest[n_cast + 1:2 * n_cast + 1]
    s_ref, p_ref = rest[2 * n_cast + 1:]
    _run_casts(cast_srcs, cast_dsts)
    n_ctx = kc_ref.shape[0]
    k = k_ref[...]
    v = v_ref[...]
    kc = kc_ref[...]
    vc = vc_ref[...]

    ts = s_ref.shape[1]
    stages = [(slice(t * ts, (t + 1) * ts), slice(g * _HEAD_DIM, (g + 1) * _HEAD_DIM))
              for t in range(q_ref.shape[0] // ts) for g in range(_GROUPS_A)]

    def scores(n):
        q = q_ref[stages[n]]
        s_ref[n % 2, :, :n_ctx] = _dot_nt(q, kc)
        s_ref[n % 2, :, n_ctx:] = _dot_nt(q, k)

    def finish(n):
        o = _softmax_pv(s_ref.at[n % 2], p_ref.at[n % 2], n_ctx, vc, v)
        o_ref[stages[n]] = o.astype(o_ref.dtype)

    scores(0)
    for n in range(len(stages)):
        if n + 1 < len(stages):
            scores(n + 1)
        finish(n)


def _attn_a(qk, pv, ka_c, pv_c, batch, seq, ctx_len, kv_heads, casts):
    wq = kv_heads * _GROUPS_A * _HEAD_DIM
    gw = _GROUPS_A * _HEAD_DIM
    tq = _pick_tile(seq, (512, 256, 128))
    ts = min(tq, 256)
    nq = seq // tq
    koff = wq // _HEAD_DIM
    n_keys = seq + ctx_len
    vmem = (2 * tq * gw * 2 * 2 + 4 * seq * _HEAD_DIM * 2 + 4 * ctx_len * _HEAD_DIM * 2
            + 2 * ts * n_keys * 6 + 3 * ts * n_keys * 4 + (6 << 20))
    grid = (batch, kv_heads, nq)
    cast_in, cast_out, cast_shapes, cast_vmem = _cast_plan(casts, grid)
    return pl.pallas_call(
        functools.partial(_attn_a_kernel, len(casts)),
        grid=grid,
        in_specs=[
            pl.BlockSpec((tq, gw), lambda b, h, i: (b * nq + i, h)),
            pl.BlockSpec((seq, _HEAD_DIM), lambda b, h, i: (b, koff + h)),
            pl.BlockSpec((seq, _HEAD_DIM), lambda b, h, i: (b, h)),
            pl.BlockSpec((ctx_len, _HEAD_DIM), lambda b, h, i: (b, h)),
            pl.BlockSpec((ctx_len, _HEAD_DIM), lambda b, h, i: (b, h)),
        ] + cast_in,
        out_specs=[pl.BlockSpec((tq, gw), lambda b, h, i: (b * nq + i, h))] + cast_out,
        out_shape=[jax.ShapeDtypeStruct((batch * seq, wq), _BF16)] + cast_shapes,
        scratch_shapes=[pltpu.VMEM((2, ts, n_keys), _F32), pltpu.VMEM((2, ts, n_keys), _BF16)],
        compiler_params=_params(vmem + cast_vmem, ("arbitrary", "arbitrary", "arbitrary")),
        name="attn_gqa",
    )(qk, qk, pv, ka_c, pv_c, *casts)


def _window_start_row(blk, grid_rows):
    lo = _BLK_ROWS * blk - _WIN_H // 2
    if isinstance(blk, int):
        return min(max(lo, 0), grid_rows - _KEY_ROWS)
    return jnp.clip(lo, 0, grid_rows - _KEY_ROWS)


def _bias_table_kernel(grid_rows, rpb_ref, o_ref):
    h = pl.program_id(0)
    n_dr = 2 * _WIN_H - 1
    n_dc = 2 * _WIN_W - 1
    shape = (_GRID_W, 2 * _GRID_W)
    qc = lax.broadcasted_iota(jnp.int32, shape, 0)
    lane = lax.broadcasted_iota(jnp.int32, shape, 1)
    col_start = jnp.clip(qc - _WIN_W // 2, 0, _GRID_W - _WIN_W)
    neg = jnp.full(shape, _NEG_INF, _F32)

    def side(kc, on_side):
        ok = on_side & (kc >= col_start) & (kc < col_start + _WIN_W)
        return kc - qc + (_WIN_W - 1), ok

    diff_l, ok_l = side(lane, lane < _GRID_W)
    diff_r, ok_r = side(lane - _GRID_W, lane >= _GRID_W)
    left, right = [], []
    for dr in range(n_dr):
        acc_l, acc_r = neg, neg
        for dc in range(n_dc):
            val = rpb_ref[(h * n_dr + dr) * n_dc + dc] * _LOG2_E
            acc_l = jnp.where(ok_l & (diff_l == dc), val, acc_l)
            acc_r = jnp.where(ok_r & (diff_r == dc), val, acc_r)
        left.append(acc_l)
        right.append(acc_r)

    n_blk = grid_rows // _BLK_ROWS
    for variant, blk in enumerate((0, 1, n_blk - 1)):
        ws = _window_start_row(blk, grid_rows)
        for a in range(_BLK_ROWS):
            qr = _BLK_ROWS * blk + a
            rs = min(max(qr - _WIN_H // 2, 0), grid_rows - _WIN_H)
            for jp in range(_KEY_ROWS // 2):
                piece = None
                for half, table in ((0, left), (1, right)):
                    kr = ws + 2 * jp + half
                    if rs <= kr < rs + _WIN_H:
                        blk_bias = table[kr - qr + _WIN_H - 1]
                        piece = blk_bias if piece is None else jnp.maximum(piece, blk_bias)
                if piece is None:
                    piece = neg
                o_ref[0, variant, a * _GRID_W:(a + 1) * _GRID_W,
                      jp * 2 * _GRID_W:(jp + 1) * 2 * _GRID_W] = piece


def _bias_tables(rpb, grid_rows):
    heads = rpb.shape[0]
    blk = (1, 3, _BLK_ROWS * _GRID_W, _KEY_ROWS * _GRID_W)
    return pl.pallas_call(
        functools.partial(_bias_table_kernel, grid_rows),
        grid=(heads,),
        in_specs=[pl.BlockSpec(memory_space=pltpu.SMEM)],
        out_specs=pl.BlockSpec(blk, lambda h: (h, 0, 0, 0)),
        out_shape=jax.ShapeDtypeStruct((heads,) + blk[1:], _F32),
        compiler_params=_params(3 * blk[1] * blk[2] * blk[3] * 4 + (8 << 20), ("arbitrary",)),
        name="nbr_bias_tables",
    )(rpb.reshape(-1))


def _attn_b_kernel(grid_rows, n_cast, q_ref, k_ref, v_ref, kc_ref, vc_ref, bias_ref, *rest):
    cast_srcs, o_ref, cast_dsts = rest[:n_cast], rest[n_cast], rest[n_cast + 1:2 * n_cast + 1]
    s_ref, p_ref = rest[2 * n_cast + 1:]
    _run_casts(cast_srcs, cast_dsts)
    n_ctx = kc_ref.shape[0]
    n_blk = grid_rows // _BLK_ROWS
    tq = _BLK_ROWS * _GRID_W
    span = _KEY_ROWS * _GRID_W
    kc = kc_ref[...]
    vc = vc_ref[...]

    def window(i):
        start = _window_start_row(i, grid_rows) * _GRID_W
        return slice(start, start + span)

    def scores(i):
        q = q_ref[i * tq:(i + 1) * tq, :]
        variant = 0 if i == 0 else (2 if i == n_blk - 1 else 1)
        s_ref[i % 2, :, :n_ctx] = _dot_nt(q, kc)
        s_ref[i % 2, :, n_ctx:] = _dot_nt(q, k_ref[window(i), :]) + bias_ref[0, variant]

    def finish(i):
        o = _softmax_pv(s_ref.at[i % 2], p_ref.at[i % 2], n_ctx, vc, v_ref[window(i), :])
        o_ref[i * tq:(i + 1) * tq, :] = o.astype(o_ref.dtype)

    scores(0)
    for i in range(n_blk):
        if i + 1 < n_blk:
            scores(i + 1)
        finish(i)


def _attn_b(pv, pv_c, bias, batch, seq, ctx_len, heads, q_off, k_off, v_off, kc_off, vc_off, casts):
    grid_rows = seq // _GRID_W
    tq = _BLK_ROWS * _GRID_W
    span = _KEY_ROWS * _GRID_W
    n_keys = span + ctx_len
    vmem = (4 * 2 * seq * _HEAD_DIM * 2 + 4 * ctx_len * _HEAD_DIM * 2
            + 2 * 3 * tq * span * 4 + 2 * tq * n_keys * 6 + 3 * tq * n_keys * 4 + (6 << 20))
    grid = (heads, batch)
    cast_in, cast_out, cast_shapes, cast_vmem = _cast_plan(casts, grid)
    return pl.pallas_call(
        functools.partial(_attn_b_kernel, grid_rows, len(casts)),
        grid=grid,
        in_specs=[
            pl.BlockSpec((seq, _HEAD_DIM), lambda h, b: (b, q_off + h)),
            pl.BlockSpec((seq, _HEAD_DIM), lambda h, b: (b, k_off + h)),
            pl.BlockSpec((seq, _HEAD_DIM), lambda h, b: (b, v_off + h)),
            pl.BlockSpec((ctx_len, _HEAD_DIM), lambda h, b: (b, kc_off + h)),
            pl.BlockSpec((ctx_len, _HEAD_DIM), lambda h, b: (b, vc_off + h)),
            pl.BlockSpec((1, 3, tq, span), lambda h, b: (h, 0, 0, 0)),
        ] + cast_in,
        out_specs=[pl.BlockSpec((seq, _HEAD_DIM), lambda h, b: (b, h))] + cast_out,
        out_shape=[jax.ShapeDtypeStruct((batch * seq, heads * _HEAD_DIM), _BF16)] + cast_shapes,
        scratch_shapes=[pltpu.VMEM((2, tq, n_keys), _F32), pltpu.VMEM((2, tq, n_keys), _BF16)],
        compiler_params=_params(vmem + cast_vmem, ("arbitrary", "arbitrary")),
        name="attn_nbr",
    )(pv, pv, pv, pv_c, pv_c, bias, *casts)


def _merge_kernel(oa_ref, ob_ref, wa_ref, wb_ref, ga_ref, gb_ref, y_ref):
    ya = _dot(oa_ref[...], wa_ref[...])
    yb = _dot(ob_ref[...], wb_ref[...])
    y = ga_ref[...].astype(_F32) * ya + gb_ref[...].astype(_F32) * yb
    y_ref[...] = y.astype(y_ref.dtype)


def _merge(o_a, o_b, w_a, w_b, gates, d_model):
    m, ka = o_a.shape
    kb = o_b.shape[1]
    tm = _pick_tile(m, (1024, 512, 256, 128))
    tn = _pick_tile(d_model, (1024, 512, 256, 128))
    nj = d_model // tn
    vmem = 2 * (tm * (ka + kb) * 2 + (ka + kb) * tn * 2 + 3 * tm * tn * 2) + 4 * tm * tn * 4 + (6 << 20)
    return pl.pallas_call(
        _merge_kernel,
        grid=(m // tm, nj),
        in_specs=[
            pl.BlockSpec((tm, ka), lambda i, j: (i, 0)),
            pl.BlockSpec((tm, kb), lambda i, j: (i, 0)),
            pl.BlockSpec((ka, tn), lambda i, j: (0, j)),
            pl.BlockSpec((kb, tn), lambda i, j: (0, j)),
            pl.BlockSpec((tm, tn), lambda i, j: (i, j)),
            pl.BlockSpec((tm, tn), lambda i, j: (i, nj + j)),
        ],
        out_specs=pl.BlockSpec((tm, tn), lambda i, j: (i, j)),
        out_shape=jax.ShapeDtypeStruct((m, d_model), _BF16),
        compiler_params=_params(vmem, ("arbitrary", "arbitrary")),
        name="branch_merge",
    )(o_a, o_b, w_a, w_b, gates, gates)


def _mm_residual_kernel(a_ref, w_ref, res_ref, gate_ref, o_ref):
    acc = _dot(a_ref[...], w_ref[...])
    o_ref[...] = res_ref[...] + gate_ref[0] * acc


def _mm_residual(a, w, res, gate, rows_per_group, tm_cands, tn_cands, name):
    m, k = a.shape
    n = w.shape[1]
    tm = _pick_tile(rows_per_group, tm_cands)
    tn = _pick_tile(n, tn_cands)
    tiles_per_group = rows_per_group // tm
    vmem = 2 * (tm * k * 2 + k * tn * 2 + 2 * tm * tn * 4) + 2 * tm * tn * 4 + (6 << 20)
    return pl.pallas_call(
        _mm_residual_kernel,
        grid=(m // tm, n // tn),
        in_specs=[
            pl.BlockSpec((tm, k), lambda i, j: (i, 0)),
            pl.BlockSpec((k, tn), lambda i, j: (0, j)),
            pl.BlockSpec((tm, tn), lambda i, j: (i, j)),
            pl.BlockSpec((1, 1, tn), lambda i, j: (i // tiles_per_group, 0, j)),
        ],
        out_specs=pl.BlockSpec((tm, tn), lambda i, j: (i, j)),
        out_shape=jax.ShapeDtypeStruct((m, n), _F32),
        compiler_params=_params(vmem, ("arbitrary", "arbitrary")),
        name=name,
    )(a, w, res, gate)


def _ffn_up_kernel(cm, h_ref, wg_ref, wv_ref, cwg_ref, cwv_ref, cbg_ref, cbv_ref, o_ref):
    rows, tn = o_ref.shape
    n_chunks = rows // cm
    wg = wg_ref[...]
    wv = wv_ref[...]
    ug, uv = [], []
    for c in range(n_chunks):
        hs = h_ref[c * cm:(c + 1) * cm, :]
        ug.append(_dot(hs, wg))
        uv.append(_dot(hs, wv))
    groups = cm // _SUBLANES
    sub = lax.broadcasted_iota(jnp.int32, (groups, _SUBLANES, tn), 1)
    zero_group = jnp.zeros((1, _SUBLANES, tn), _F32)

    def rotations(u):
        u3 = [x.reshape(groups, _SUBLANES, tn) for x in u]
        down = [pltpu.roll(x, 1, axis=1) for x in u3]
        up = [pltpu.roll(x, _SUBLANES - 1, axis=1) for x in u3]
        return u3, down, up

    def conv(rot, c, cw_ref, cb_ref):
        u3, down, up = rot
        before = down[c - 1][groups - 1:] if c > 0 else zero_group
        after = up[c + 1][:1] if c + 1 < n_chunks else zero_group
        prev = jnp.where(sub == 0, jnp.concatenate([before, down[c][:groups - 1]], axis=0), down[c])
        nxt = jnp.where(sub == _SUBLANES - 1, jnp.concatenate([up[c][1:], after], axis=0), up[c])
        out = prev * cw_ref[0:1, :] + u3[c] * cw_ref[1:2, :] + nxt * cw_ref[2:3, :] + cb_ref[...]
        return out.reshape(cm, tn)

    rot_g = rotations(ug)
    rot_v = rotations(uv)
    for c in range(n_chunks):
        gate = conv(rot_g, c, cwg_ref, cbg_ref)
        val = conv(rot_v, c, cwv_ref, cbv_ref)
        o_ref[c * cm:(c + 1) * cm, :] = (gate * jax.nn.sigmoid(gate) * val).astype(o_ref.dtype)


def _ffn_up(h2, w_up, conv_w, conv_b, seq):
    m, d = h2.shape
    d_ff = w_up.shape[1] // 2
    tn = _pick_tile(d_ff, (256, 128))
    nj = d_ff // tn
    cm = _pick_tile(seq, (512, 256, 128, 64, 32, 16))
    taps = conv_w.shape[0]
    vmem = seq * d * 2 + 2 * 2 * d * tn * 2 + 2 * seq * tn * 2 + 6 * seq * tn * 4 + (6 << 20)
    return pl.pallas_call(
        functools.partial(_ffn_up_kernel, cm),
        grid=(m // seq, nj),
        in_specs=[
            pl.BlockSpec((seq, d), lambda b, j: (b, 0), pipeline_mode=pl.Buffered(1)),
            pl.BlockSpec((d, tn), lambda b, j: (0, j)),
            pl.BlockSpec((d, tn), lambda b, j: (0, nj + j)),
            pl.BlockSpec((taps, tn), lambda b, j: (0, j)),
            pl.BlockSpec((taps, tn), lambda b, j: (0, nj + j)),
            pl.BlockSpec((1, tn), lambda b, j: (0, j)),
            pl.BlockSpec((1, tn), lambda b, j: (0, nj + j)),
        ],
        out_specs=pl.BlockSpec((seq, tn), lambda b, j: (b, j)),
        out_shape=jax.ShapeDtypeStruct((m, d_ff), _BF16),
        compiler_params=_params(vmem, ("arbitrary", "arbitrary")),
        name="ffn_up_conv_gate",
    )(h2, w_up, w_up, conv_w, conv_w, conv_b, conv_b)


def kernel(x, c, ctx, c_ctx, ada_w, ada_b, norm1_g, w_in, qnorm_a, knorm_a, rpb_b,
           w_br_a, w_br_b, w_out, norm2_g, w_up, conv_w, conv_b, w_down, final_g):
    batch, seq, d = x.shape
    ctx_len = ctx.shape[1]
    depth = ada_w.shape[0]
    wa_q = w_br_a.shape[1]
    wb = w_br_b.shape[1]
    wa_kv = (w_in.shape[2] - wa_q - 3 * wb - 2 * d) // 2
    kv_heads = wa_kv // _HEAD_DIM
    heads_b = wb // _HEAD_DIM
    assert depth == 1, "the context-stream update between layers is not implemented"
    assert wa_q == kv_heads * _GROUPS_A * _HEAD_DIM and seq % (_BLK_ROWS * _GRID_W) == 0
    assert seq // _GRID_W >= _KEY_ROWS and qnorm_a.shape[-1] == _HEAD_DIM
    grid_rows = seq // _GRID_W
    m = batch * seq
    xf = x.reshape(m, d)
    ctxf = ctx.reshape(batch * ctx_len, d)

    pad_rows = -(batch + 1) % 16
    cc = jnp.concatenate([c, c_ctx[None, :], jnp.zeros((pad_rows, d), _F32)], axis=0)
    mod = _modulation(cc, ada_w[0], ada_b[0][None, :])
    sh1, sc1, g1, sh2, sc2, g2 = [mod[:batch, i * d:(i + 1) * d][:, None, :] for i in range(6)]
    csh1, csc1 = [mod[batch:batch + 1, i * d:(i + 1) * d][:, None, :] for i in range(2)]

    w_in16 = w_in[0].astype(_BF16)
    n_in = w_in16.shape[1]
    o_k = wa_q
    o_v = o_k + wa_kv
    o_qb = o_v + wa_kv
    o_kb = o_qb + wb
    o_g = o_qb + 3 * wb

    g1n = norm1_g[0][None, :]
    h = _rmsnorm(xf, g1n, sc1, sh1, seq, _BF16)
    hc = _rmsnorm(ctxf, g1n, csc1, csh1, batch * ctx_len, _BF16)

    cos_t, sin_t = _rope_tables(seq)
    qk_cols = [(0, o_v)]
    tn_qk = _inproj_tile(qk_cols)
    assert wa_q % tn_qk == 0
    gain_q = jnp.broadcast_to(qnorm_a[0] * _QK_SCALE, (wa_q // tn_qk, 1, _HEAD_DIM))
    gain_k = jnp.broadcast_to(knorm_a[0], (wa_kv // tn_qk, 1, _HEAD_DIM))
    qk = _inproj(h, w_in16, qk_cols, "normrope", seq, jnp.concatenate([gain_q, gain_k], axis=0), cos_t, sin_t)
    pv_cols = [(o_v, o_g - o_v)]
    tn_pv = _inproj_tile(pv_cols)
    assert wa_kv % tn_pv == 0 and wb % tn_pv == 0
    pv_tile = jnp.arange((o_g - o_v) // tn_pv)
    is_qb = (pv_tile >= wa_kv // tn_pv) & (pv_tile < (wa_kv + wb) // tn_pv)
    pv_scale = jnp.broadcast_to(jnp.where(is_qb, _QK_SCALE, 1.0).astype(_F32)[:, None, None],
                                (pv_tile.shape[0], 1, _HEAD_DIM))
    pv = _inproj(h, w_in16, pv_cols, "scaled", seq, pv_scale)
    gates = _inproj(h, w_in16, [(o_g, n_in - o_g)], "sigmoid", seq)
    c_rows = batch * ctx_len
    kc_cols = [(o_k, wa_kv)]
    gain_kc = jnp.broadcast_to(knorm_a[0], (wa_kv // _inproj_tile(kc_cols), 1, _HEAD_DIM))
    ka_c = _inproj(hc, w_in16, kc_cols, "normrope", c_rows, gain_kc,
                   jnp.ones((c_rows, _HEAD_DIM), _F32), jnp.zeros((c_rows, _HEAD_DIM), _F32))
    pv_c = _inproj(hc, w_in16, [(o_v, wa_kv), (o_kb, 2 * wb)], "plain", c_rows)

    o_a, w_up16, w_down16 = _attn_a(qk, pv, ka_c, pv_c, batch, seq, ctx_len, kv_heads,
                                    casts=[w_up[0], w_down[0]])
    bias = _bias_tables(rpb_b[0], grid_rows)
    o_b, w_bra16, w_brb16, w_out16 = _attn_b(
        pv, pv_c, bias, batch, seq, ctx_len, heads_b,
        q_off=kv_heads, k_off=kv_heads + heads_b, v_off=kv_heads + 2 * heads_b,
        kc_off=kv_heads, vc_off=kv_heads + heads_b, casts=[w_br_a[0], w_br_b[0], w_out[0]])

    y = _merge(o_a, o_b, w_bra16, w_brb16, gates, d)
    x1 = _mm_residual(y, w_out16, xf, g1, seq,
                      (1024, 512, 256, 128), (1024, 512, 256, 128), "out_proj_residual")

    h2 = _rmsnorm(x1, norm2_g[0][None, :], sc2, sh2, seq, _BF16)
    t = _ffn_up(h2, w_up16, conv_w[0], conv_b[0][None, :], seq)
    x2 = _mm_residual(t, w_down16, x1, g2, seq,
                      (512, 256, 128), (512, 256, 128), "ffn_down_residual")

    out = _rmsnorm(x2, final_g[None, :], None, None, seq, _F32)
    return out.reshape(batch, seq, d)
```

```python
import functools
import math

import jax
import jax.numpy as jnp
from jax import lax
from jax.experimental import pallas as pl
from jax.experimental.pallas import tpu as pltpu

_HEAD_DIM = 128
_GRID_W = 64
_WIN_H = 8
_WIN_W = 16
_GROUPS_A = 4
_ROPE_THETA = 10000.0
_EPS = 1e-6
_BLK_ROWS = 4
_KEY_ROWS = _BLK_ROWS + _WIN_H
_SUBLANES = 8
_V7X_VMEM_BYTES = 64 * 1024 * 1024
_VMEM_CAP = _V7X_VMEM_BYTES - 6 * 1024 * 1024
_NEG_INF = float("-inf")
_LOG2_E = 1.4426950408889634
_QK_SCALE = _LOG2_E * _HEAD_DIM ** -0.5

_BF16 = jnp.bfloat16
_F32 = jnp.float32


def _params(vmem_bytes, semantics):
    limit = int(min(max(vmem_bytes, 32 * 1024 * 1024), _VMEM_CAP))
    return pltpu.CompilerParams(dimension_semantics=semantics, vmem_limit_bytes=limit)


def _dot(a, b):
    return jnp.dot(a, b, preferred_element_type=_F32)


def _dot_nt(a, b):
    return lax.dot_general(a, b, (((1,), (1,)), ((), ())), preferred_element_type=_F32)


def _pick_tile(n, candidates):
    for t in candidates:
        if n % t == 0:
            return t
    raise ValueError(f"no tile in {candidates} divides {n}")


def _cast_plan(weights, grid):
    n_steps = math.prod(grid)

    def step(*ids):
        s = ids[0]
        for n, i in zip(grid[1:], ids[1:]):
            s = s * n + i
        return s

    in_specs, out_specs, out_shapes, vmem = [], [], [], 0
    for w in weights:
        r, c = w.shape
        br = next(b for b in range(16, r + 1, 16) if r % b == 0 and r // b <= n_steps)
        last = r // br - 1

        def index(*ids, last=last):
            return jnp.minimum(step(*ids), last), 0

        in_specs.append(pl.BlockSpec((br, c), index))
        out_specs.append(pl.BlockSpec((br, c), index))
        out_shapes.append(jax.ShapeDtypeStruct((r, c), _BF16))
        vmem += 2 * br * c * (4 + 2) + br * c * 4
    return in_specs, out_specs, out_shapes, vmem


def _run_casts(srcs, dsts):
    for src, dst in zip(srcs, dsts):
        dst[...] = src[...].astype(dst.dtype)


def _mod_kernel(c_ref, w_ref, b_ref, o_ref):
    c = c_ref[...]
    s = (c * jax.nn.sigmoid(c)).astype(_BF16)
    o_ref[...] = _dot(s, w_ref[...].astype(_BF16)) + b_ref[...]


def _modulation(cc, ada_w, ada_b):
    rows, d = cc.shape
    n = ada_w.shape[1]
    tn = _pick_tile(n, (512, 256, 128))
    vmem = 2 * (d * tn * 4) + d * tn * 2 + 4 * rows * d * 4 + (4 << 20)
    return pl.pallas_call(
        _mod_kernel,
        grid=(n // tn,),
        in_specs=[
            pl.BlockSpec((rows, d), lambda j: (0, 0)),
            pl.BlockSpec((d, tn), lambda j: (0, j)),
            pl.BlockSpec((1, tn), lambda j: (0, j)),
        ],
        out_specs=pl.BlockSpec((rows, tn), lambda j: (0, j)),
        out_shape=jax.ShapeDtypeStruct((rows, n), _F32),
        compiler_params=_params(vmem, ("arbitrary",)),
        name="adaln_mod",
    )(cc, ada_w, ada_b)


def _norm_kernel(modulate, x_ref, g_ref, *rest):
    if modulate:
        sc_ref, sh_ref, o_ref = rest
    else:
        (o_ref,) = rest
    x = x_ref[...]
    ms = jnp.mean(x * x, axis=-1, keepdims=True)
    y = x * lax.rsqrt(ms + _EPS) * g_ref[...]
    if modulate:
        y = y * (1.0 + sc_ref[0]) + sh_ref[0]
    o_ref[...] = y.astype(o_ref.dtype)


def _rmsnorm(x, g, sc, sh, rows_per_group, out_dtype):
    m, d = x.shape
    tm = _pick_tile(rows_per_group, (512, 256, 128, 64, 32, 16, 8))
    tiles_per_group = rows_per_group // tm
    modulate = sc is not None
    in_specs = [pl.BlockSpec((tm, d), lambda i: (i, 0)), pl.BlockSpec((1, d), lambda i: (0, 0))]
    args = [x, g]
    if modulate:
        mod_spec = pl.BlockSpec((1, 1, d), lambda i: (i // tiles_per_group, 0, 0))
        in_specs += [mod_spec, mod_spec]
        args += [sc, sh]
    vmem = 2 * tm * d * 4 + 2 * tm * d * jnp.dtype(out_dtype).itemsize + 3 * tm * d * 4 + (4 << 20)
    return pl.pallas_call(
        functools.partial(_norm_kernel, modulate),
        grid=(m // tm,),
        in_specs=in_specs,
        out_specs=pl.BlockSpec((tm, d), lambda i: (i, 0)),
        out_shape=jax.ShapeDtypeStruct((m, d), out_dtype),
        compiler_params=_params(vmem, ("arbitrary",)),
        name="rmsnorm_mod" if modulate else "rmsnorm",
    )(*args)


def _rope_table_kernel(freq_ref, cos_ref, sin_ref):
    shape = cos_ref.shape
    t = lax.broadcasted_iota(jnp.int32, shape, 0)
    lane = lax.broadcasted_iota(jnp.int32, shape, 1)
    row = lax.shift_right_logical(t, _GRID_W.bit_length() - 1)
    col = jnp.bitwise_and(t, _GRID_W - 1)
    pos = jnp.where(lane < _HEAD_DIM // 2, row, col).astype(_F32)
    ang = pos * freq_ref[...]
    first_half = jnp.bitwise_and(lane, _HEAD_DIM // 2 - 1) < _HEAD_DIM // 4
    cos_ref[...] = jnp.cos(ang)
    sin_ref[...] = jnp.where(first_half, -jnp.sin(ang), jnp.sin(ang))


def _rope_tables(seq):
    r = _HEAD_DIM // 4
    freqs = _ROPE_THETA ** (-jnp.arange(r, dtype=_F32) / r)
    freq_lanes = jnp.tile(freqs, _HEAD_DIM // r)[None, :]
    out = jax.ShapeDtypeStruct((seq, _HEAD_DIM), _F32)
    return pl.pallas_call(
        _rope_table_kernel,
        out_shape=(out, out),
        name="rope_tables",
    )(freq_lanes)


def _inproj_kernel(kind, cm, h_ref, w_ref, *rest):
    if kind == "normrope":
        gain_ref, cos_ref, sin_ref, o_ref = rest
    elif kind == "scaled":
        gain_ref, o_ref = rest
    else:
        (o_ref,) = rest
    rows, tn = o_ref.shape
    w = w_ref[...]
    if kind in ("normrope", "scaled"):
        gain = gain_ref[0]
    if kind == "normrope":
        lane = lax.broadcasted_iota(jnp.int32, (cm, _HEAD_DIM), 1)
        first_half = jnp.bitwise_and(lane, _HEAD_DIM // 2 - 1) < _HEAD_DIM // 4
    for c in range(rows // cm):
        rs = slice(c * cm, (c + 1) * cm)
        acc = _dot(h_ref[rs, :], w)
        if kind == "plain":
            o_ref[rs, :] = acc.astype(o_ref.dtype)
        elif kind == "scaled":
            o_ref[rs, :] = (acc * gain[:, :1]).astype(o_ref.dtype)
        elif kind == "sigmoid":
            o_ref[rs, :] = jax.nn.sigmoid(acc).astype(o_ref.dtype)
        else:
            cos = cos_ref[rs, :]
            sin = sin_ref[rs, :]
            for hc in range(tn // _HEAD_DIM):
                cs = slice(hc * _HEAD_DIM, (hc + 1) * _HEAD_DIM)
                xh = acc[:, cs]
                ms = jnp.mean(xh * xh, axis=-1, keepdims=True)
                y = xh * lax.rsqrt(ms + _EPS) * gain
                partner = jnp.where(
                    first_half,
                    pltpu.roll(y, _HEAD_DIM - _HEAD_DIM // 4, axis=1),
                    pltpu.roll(y, _HEAD_DIM // 4, axis=1),
                )
                o_ref[rs, cs] = (y * cos + partner * sin).astype(o_ref.dtype)


def _inproj_tile(col_ranges):
    edges = [v for start_width in col_ranges for v in start_width]
    return _pick_tile(functools.reduce(math.gcd, edges), (512, 256, 128))


def _inproj(h, w, col_ranges, kind, slab_rows, gains=None, cos=None, sin=None):
    m, d = h.shape
    n = sum(width for _, width in col_ranges)
    tn = _inproj_tile(col_ranges)
    cm = _pick_tile(slab_rows, (512, 256, 128, 64, 32, 16))

    def w_block(j):
        blk, first = None, 0
        for start, width in col_ranges:
            here = start // tn + (j - first)
            blk = here if blk is None else jnp.where(j >= first, here, blk)
            first += width // tn
        return blk

    in_specs = [
        pl.BlockSpec((slab_rows, d), lambda b, j: (b, 0), pipeline_mode=pl.Buffered(1)),
        pl.BlockSpec((d, tn), lambda b, j: (0, w_block(j))),
    ]
    args = [h, w]
    vmem = slab_rows * d * 2 + 2 * d * tn * 2 + 2 * slab_rows * tn * 2 + 6 * cm * tn * 4 + (6 << 20)
    if kind in ("normrope", "scaled"):
        in_specs.append(pl.BlockSpec((1, 1, _HEAD_DIM), lambda b, j: (j, 0, 0)))
        args.append(gains)
    if kind == "normrope":
        in_specs += [
            pl.BlockSpec((slab_rows, _HEAD_DIM), lambda b, j: (0, 0), pipeline_mode=pl.Buffered(1)),
            pl.BlockSpec((slab_rows, _HEAD_DIM), lambda b, j: (0, 0), pipeline_mode=pl.Buffered(1)),
        ]
        args += [cos, sin]
        vmem += 2 * slab_rows * _HEAD_DIM * 4
    return pl.pallas_call(
        functools.partial(_inproj_kernel, kind, cm),
        grid=(m // slab_rows, n // tn),
        in_specs=in_specs,
        out_specs=pl.BlockSpec((slab_rows, tn), lambda b, j: (b, j)),
        out_shape=jax.ShapeDtypeStruct((m, n), _BF16),
        compiler_params=_params(vmem, ("arbitrary", "arbitrary")),
        name="inproj_" + kind,
    )(*args)


def _softmax_pv(s_ref, p_ref, n_ctx, vc, v):
    s = s_ref[...]
    m = jnp.max(s, axis=-1, keepdims=True)
    p = jnp.exp2(s - m)
    denom = jnp.sum(p, axis=-1, keepdims=True)
    p_ref[...] = p.astype(p_ref.dtype)
    o = _dot(p_ref[:, :n_ctx], vc) + _dot(p_ref[:, n_ctx:], v)
    return o / denom


def _attn_a_kernel(n_cast, q_ref, k_ref, v_ref, kc_ref, vc_ref, *rest):
    cast_srcs, o_ref, cast_dsts = rest[:n_cast], rest[n_cast], rest[n_cast + 1:2 * n_cast + 1]
    s_ref, p_ref = rest[2 * n_cast + 1:]
    _run_casts(cast_srcs, cast_dsts)
    n_ctx = kc_ref.shape[0]
    k = k_ref[...]
    v = v_ref[...]
    kc = kc_ref[...]
    vc = vc_ref[...]

    ts = s_ref.shape[1]
    stages = [(slice(t * ts, (t + 1) * ts), slice(g * _HEAD_DIM, (g + 1) * _HEAD_DIM))
              for t in range(q_ref.shape[0] // ts) for g in range(_GROUPS_A)]

    def scores(n):
        q = q_ref[stages[n]]
        s_ref[n % 2, :, :n_ctx] = _dot_nt(q, kc)
        s_ref[n % 2, :, n_ctx:] = _dot_nt(q, k)

    def finish(n):
        o = _softmax_pv(s_ref.at[n % 2], p_ref.at[n % 2], n_ctx, vc, v)
        o_ref[stages[n]] = o.astype(o_ref.dtype)

    scores(0)
    for n in range(len(stages)):
        if n + 1 < len(stages):
            scores(n + 1)
        finish(n)


def _attn_a(qk, pv, ka_c, pv_c, batch, seq, ctx_len, kv_heads, casts):
    wq = kv_heads * _GROUPS_A * _HEAD_DIM
    gw = _GROUPS_A * _HEAD_DIM
    tq = _pick_tile(seq, (512, 256, 128))
    ts = min(tq, 256)
    nq = seq // tq
    koff = wq // _HEAD_DIM
    n_keys = seq + ctx_len
    vmem = (2 * tq * gw * 2 * 2 + 4 * seq * _HEAD_DIM * 2 + 4 * ctx_len * _HEAD_DIM * 2
            + 2 * ts * n_keys * 6 + 3 * ts * n_keys * 4 + (6 << 20))
    grid = (batch, kv_heads, nq)
    cast_in, cast_out, cast_shapes, cast_vmem = _cast_plan(casts, grid)
    return pl.pallas_call(
        functools.partial(_attn_a_kernel, len(casts)),
        grid=grid,
        in_specs=[
            pl.BlockSpec((tq, gw), lambda b, h, i: (b * nq + i, h)),
            pl.BlockSpec((seq, _HEAD_DIM), lambda b, h, i: (b, koff + h)),
            pl.BlockSpec((seq, _HEAD_DIM), lambda b, h, i: (b, h)),
            pl.BlockSpec((ctx_len, _HEAD_DIM), lambda b, h, i: (b, h)),
            pl.BlockSpec((ctx_len, _HEAD_DIM), lambda b, h, i: (b, h)),
        ] + cast_in,
        out_specs=[pl.BlockSpec((tq, gw), lambda b, h, i: (b * nq + i, h))] + cast_out,
        out_shape=[jax.ShapeDtypeStruct((batch * seq, wq), _BF16)] + cast_shapes,
        scratch_shapes=[pltpu.VMEM((2, ts, n_keys), _F32), pltpu.VMEM((2, ts, n_keys), _BF16)],
        compiler_params=_params(vmem + cast_vmem, ("arbitrary", "arbitrary", "arbitrary")),
        name="attn_gqa",
    )(qk, qk, pv, ka_c, pv_c, *casts)


def _window_start_row(blk, grid_rows):
    lo = _BLK_ROWS * blk - _WIN_H // 2
    if isinstance(blk, int):
        return min(max(lo, 0), grid_rows - _KEY_ROWS)
    return jnp.clip(lo, 0, grid_rows - _KEY_ROWS)


def _bias_table_kernel(grid_rows, rpb_ref, o_ref):
    h = pl.program_id(0)
    n_dr = 2 * _WIN_H - 1
    n_dc = 2 * _WIN_W - 1
    shape = (_GRID_W, 2 * _GRID_W)
    qc = lax.broadcasted_iota(jnp.int32, shape, 0)
    lane = lax.broadcasted_iota(jnp.int32, shape, 1)
    col_start = jnp.clip(qc - _WIN_W // 2, 0, _GRID_W - _WIN_W)
    neg = jnp.full(shape, _NEG_INF, _F32)

    def side(kc, on_side):
        ok = on_side & (kc >= col_start) & (kc < col_start + _WIN_W)
        return kc - qc + (_WIN_W - 1), ok

    diff_l, ok_l = side(lane, lane < _GRID_W)
    diff_r, ok_r = side(lane - _GRID_W, lane >= _GRID_W)
    left, right = [], []
    for dr in range(n_dr):
        acc_l, acc_r = neg, neg
        for dc in range(n_dc):
            val = rpb_ref[(h * n_dr + dr) * n_dc + dc] * _LOG2_E
            acc_l = jnp.where(ok_l & (diff_l == dc), val, acc_l)
            acc_r = jnp.where(ok_r & (diff_r == dc), val, acc_r)
        left.append(acc_l)
        right.append(acc_r)

    n_blk = grid_rows // _BLK_ROWS
    for variant, blk in enumerate((0, 1, n_blk - 1)):
        ws = _window_start_row(blk, grid_rows)
        for a in range(_BLK_ROWS):
            qr = _BLK_ROWS * blk + a
            rs = min(max(qr - _WIN_H // 2, 0), grid_rows - _WIN_H)
            for jp in range(_KEY_ROWS // 2):
                piece = None
                for half, table in ((0, left), (1, right)):
                    kr = ws + 2 * jp + half
                    if rs <= kr < rs + _WIN_H:
                        blk_bias = table[kr - qr + _WIN_H - 1]
                        piece = blk_bias if piece is None else jnp.maximum(piece, blk_bias)
                if piece is None:
                    piece = neg
                o_ref[0, variant, a * _GRID_W:(a + 1) * _GRID_W,
                      jp * 2 * _GRID_W:(jp + 1) * 2 * _GRID_W] = piece


def _bias_tables(rpb, grid_rows):
    heads = rpb.shape[0]
    blk = (1, 3, _BLK_ROWS * _GRID_W, _KEY_ROWS * _GRID_W)
    return pl.pallas_call(
        functools.partial(_bias_table_kernel, grid_rows),
        grid=(heads,),
        in_specs=[pl.BlockSpec(memory_space=pltpu.SMEM)],
        out_specs=pl.BlockSpec(blk, lambda h: (h, 0, 0, 0)),
        out_shape=jax.ShapeDtypeStruct((heads,) + blk[1:], _F32),
        compiler_params=_params(3 * blk[1] * blk[2] * blk[3] * 4 + (8 << 20), ("arbitrary",)),
        name="nbr_bias_tables",
    )(rpb.reshape(-1))


def _attn_b_kernel(grid_rows, n_cast, q_ref, k_ref, v_ref, kc_ref, vc_ref, bias_ref, *rest):
    cast_srcs, o_ref, cast_dsts = rest[:n_cast], rest[n_cast], rest[n_cast + 1:2 * n_cast + 1]
    s_ref, p_ref = rest[2 * n_cast + 1:]
    _run_casts(cast_srcs, cast_dsts)
    n_ctx = kc_ref.shape[0]
    n_blk = grid_rows // _BLK_ROWS
    tq = _BLK_ROWS * _GRID_W
    span = _KEY_ROWS * _GRID_W
    kc = kc_ref[...]
    vc = vc_ref[...]

    def window(i):
        start = _window_start_row(i, grid_rows) * _GRID_W
        return slice(start, start + span)

    def scores(i):
        q = q_ref[i * tq:(i + 1) * tq, :]
        variant = 0 if i == 0 else (2 if i == n_blk - 1 else 1)
        s_ref[i % 2, :, :n_ctx] = _dot_nt(q, kc)
        s_ref[i % 2, :, n_ctx:] = _dot_nt(q, k_ref[window(i), :]) + bias_ref[0, variant]

    def finish(i):
        o = _softmax_pv(s_ref.at[i % 2], p_ref.at[i % 2], n_ctx, vc, v_ref[window(i), :])
        o_ref[i * tq:(i + 1) * tq, :] = o.astype(o_ref.dtype)

    scores(0)
    for i in range(n_blk):
        if i + 1 < n_blk:
            scores(i + 1)
        finish(i)


def _attn_b(pv, pv_c, bias, batch, seq, ctx_len, heads, q_off, k_off, v_off, kc_off, vc_off, casts):
    grid_rows = seq // _GRID_W
    tq = _BLK_ROWS * _GRID_W
    span = _KEY_ROWS * _GRID_W
    n_keys = span + ctx_len
    vmem = (4 * 2 * seq * _HEAD_DIM * 2 + 4 * ctx_len * _HEAD_DIM * 2
            + 2 * 3 * tq * span * 4 + 2 * tq * n_keys * 6 + 3 * tq * n_keys * 4 + (6 << 20))
    grid = (heads, batch)
    cast_in, cast_out, cast_shapes, cast_vmem = _cast_plan(casts, grid)
    return pl.pallas_call(
        functools.partial(_attn_b_kernel, grid_rows, len(casts)),
        grid=grid,
        in_specs=[
            pl.BlockSpec((seq, _HEAD_DIM), lambda h, b: (b, q_off + h)),
            pl.BlockSpec((seq, _HEAD_DIM), lambda h, b: (b, k_off + h)),
            pl.BlockSpec((seq, _HEAD_DIM), lambda h, b: (b, v_off + h)),
            pl.BlockSpec((ctx_len, _HEAD_DIM), lambda h, b: (b, kc_off + h)),
            pl.BlockSpec((ctx_len, _HEAD_DIM), lambda h, b: (b, vc_off + h)),
            pl.BlockSpec((1, 3, tq, span), lambda h, b: (h, 0, 0, 0)),
        ] + cast_in,
        out_specs=[pl.BlockSpec((seq, _HEAD_DIM), lambda h, b: (b, h))] + cast_out,
        out_shape=[jax.ShapeDtypeStruct((batch * seq, heads * _HEAD_DIM), _BF16)] + cast_shapes,
        scratch_shapes=[pltpu.VMEM((2, tq, n_keys), _F32), pltpu.VMEM((2, tq, n_keys), _BF16)],
        compiler_params=_params(vmem + cast_vmem, ("arbitrary", "arbitrary")),
        name="attn_nbr",
    )(pv, pv, pv, pv_c, pv_c, bias, *casts)


def _merge_kernel(oa_ref, ob_ref, wa_ref, wb_ref, ga_ref, gb_ref, y_ref):
    ya = _dot(oa_ref[...], wa_ref[...])
    yb = _dot(ob_ref[...], wb_ref[...])
    y = ga_ref[...].astype(_F32) * ya + gb_ref[...].astype(_F32) * yb
    y_ref[...] = y.astype(y_ref.dtype)


def _merge(o_a, o_b, w_a, w_b, gates, d_model):
    m, ka = o_a.shape
    kb = o_b.shape[1]
    tm = _pick_tile(m, (1024, 512, 256, 128))
    tn = _pick_tile(d_model, (1024, 512, 256, 128))
    nj = d_model // tn
    vmem = 2 * (tm * (ka + kb) * 2 + (ka + kb) * tn * 2 + 3 * tm * tn * 2) + 4 * tm * tn * 4 + (6 << 20)
    return pl.pallas_call(
        _merge_kernel,
        grid=(m // tm, nj),
        in_specs=[
            pl.BlockSpec((tm, ka), lambda i, j: (i, 0)),
            pl.BlockSpec((tm, kb), lambda i, j: (i, 0)),
            pl.BlockSpec((ka, tn), lambda i, j: (0, j)),
            pl.BlockSpec((kb, tn), lambda i, j: (0, j)),
            pl.BlockSpec((tm, tn), lambda i, j: (i, j)),
            pl.BlockSpec((tm, tn), lambda i, j: (i, nj + j)),
        ],
        out_specs=pl.BlockSpec((tm, tn), lambda i, j: (i, j)),
        out_shape=jax.ShapeDtypeStruct((m, d_model), _BF16),
        compiler_params=_params(vmem, ("arbitrary", "arbitrary")),
        name="branch_merge",
    )(o_a, o_b, w_a, w_b, gates, gates)


def _mm_residual_kernel(a_ref, w_ref, res_ref, gate_ref, o_ref):
    acc = _dot(a_ref[...], w_ref[...])
    o_ref[...] = res_ref[...] + gate_ref[0] * acc


def _mm_residual(a, w, res, gate, rows_per_group, tm_cands, tn_cands, name):
    m, k = a.shape
    n = w.shape[1]
    tm = _pick_tile(rows_per_group, tm_cands)
    tn = _pick_tile(n, tn_cands)
    tiles_per_group = rows_per_group // tm
    vmem = 2 * (tm * k * 2 + k * tn * 2 + 2 * tm * tn * 4) + 2 * tm * tn * 4 + (6 << 20)
    return pl.pallas_call(
        _mm_residual_kernel,
        grid=(m // tm, n // tn),
        in_specs=[
            pl.BlockSpec((tm, k), lambda i, j: (i, 0)),
            pl.BlockSpec((k, tn), lambda i, j: (0, j)),
            pl.BlockSpec((tm, tn), lambda i, j: (i, j)),
            pl.BlockSpec((1, 1, tn), lambda i, j: (i // tiles_per_group, 0, j)),
        ],
        out_specs=pl.BlockSpec((tm, tn), lambda i, j: (i, j)),
        out_shape=jax.ShapeDtypeStruct((m, n), _F32),
        compiler_params=_params(vmem, ("arbitrary", "arbitrary")),
        name=name,
    )(a, w, res, gate)


def _ffn_up_kernel(cm, h_ref, wg_ref, wv_ref, cwg_ref, cwv_ref, cbg_ref, cbv_ref, o_ref):
    rows, tn = o_ref.shape
    n_chunks = rows // cm
    wg = wg_ref[...]
    wv = wv_ref[...]
    ug, uv = [], []
    for c in range(n_chunks):
        hs = h_ref[c * cm:(c + 1) * cm, :]
        ug.append(_dot(hs, wg))
        uv.append(_dot(hs, wv))
    groups = cm // _SUBLANES
    sub = lax.broadcasted_iota(jnp.int32, (groups, _SUBLANES, tn), 1)
    zero_group = jnp.zeros((1, _SUBLANES, tn), _F32)

    def conv(u, c, cw_ref, cb_ref):
        u3 = [x.reshape(groups, _SUBLANES, tn) for x in u]
        before = u3[c - 1][groups - 1:] if c > 0 else zero_group
        after = u3[c + 1][:1] if c + 1 < n_chunks else zero_group
        above = jnp.concatenate([before, u3[c][:groups - 1]], axis=0)
        below = jnp.concatenate([u3[c][1:], after], axis=0)
        prev = pltpu.roll(jnp.where(sub == _SUBLANES - 1, above, u3[c]), 1, axis=1)
        nxt = pltpu.roll(jnp.where(sub == 0, below, u3[c]), _SUBLANES - 1, axis=1)
        out = prev * cw_ref[0:1, :] + u3[c] * cw_ref[1:2, :] + nxt * cw_ref[2:3, :] + cb_ref[...]
        return out.reshape(cm, tn)

    for c in range(n_chunks):
        gate = conv(ug, c, cwg_ref, cbg_ref)
        val = conv(uv, c, cwv_ref, cbv_ref)
        o_ref[c * cm:(c + 1) * cm, :] = (gate * jax.nn.sigmoid(gate) * val).astype(o_ref.dtype)


def _ffn_up(h2, w_up, conv_w, conv_b, seq):
    m, d = h2.shape
    d_ff = w_up.shape[1] // 2
    tn = _pick_tile(d_ff, (256, 128))
    nj = d_ff // tn
    cm = _pick_tile(seq, (512, 256, 128, 64, 32, 16))
    taps = conv_w.shape[0]
    vmem = seq * d * 2 + 2 * 2 * d * tn * 2 + 2 * seq * tn * 2 + 6 * seq * tn * 4 + (6 << 20)
    return pl.pallas_call(
        functools.partial(_ffn_up_kernel, cm),
        grid=(m // seq, nj),
        in_specs=[
            pl.BlockSpec((seq, d), lambda b, j: (b, 0), pipeline_mode=pl.Buffered(1)),
            pl.BlockSpec((d, tn), lambda b, j: (0, j)),
            pl.BlockSpec((d, tn), lambda b, j: (0, nj + j)),
            pl.BlockSpec((taps, tn), lambda b, j: (0, j)),
            pl.BlockSpec((taps, tn), lambda b, j: (0, nj + j)),
            pl.BlockSpec((1, tn), lambda b, j: (0, j)),
            pl.BlockSpec((1, tn), lambda b, j: (0, nj + j)),
        ],
        out_specs=pl.BlockSpec((seq, tn), lambda b, j: (b, j)),
        out_shape=jax.ShapeDtypeStruct((m, d_ff), _BF16),
        compiler_params=_params(vmem, ("arbitrary", "arbitrary")),
        name="ffn_up_conv_gate",
    )(h2, w_up, w_up, conv_w, conv_w, conv_b, conv_b)


def kernel(x, c, ctx, c_ctx, ada_w, ada_b, norm1_g, w_in, qnorm_a, knorm_a, rpb_b,
           w_br_a, w_br_b, w_out, norm2_g, w_up, conv_w, conv_b, w_down, final_g):
    batch, seq, d = x.shape
    ctx_len = ctx.shape[1]
    depth = ada_w.shape[0]
    wa_q = w_br_a.shape[1]
    wb = w_br_b.shape[1]
    wa_kv = (w_in.shape[2] - wa_q - 3 * wb - 2 * d) // 2
    kv_heads = wa_kv // _HEAD_DIM
    heads_b = wb // _HEAD_DIM
    assert depth == 1, "the context-stream update between layers is not implemented"
    assert wa_q == kv_heads * _GROUPS_A * _HEAD_DIM and seq % (_BLK_ROWS * _GRID_W) == 0
    assert seq // _GRID_W >= _KEY_ROWS and qnorm_a.shape[-1] == _HEAD_DIM
    grid_rows = seq // _GRID_W
    m = batch * seq
    xf = x.reshape(m, d)
    ctxf = ctx.reshape(batch * ctx_len, d)

    pad_rows = -(batch + 1) % 16
    cc = jnp.concatenate([c, c_ctx[None, :], jnp.zeros((pad_rows, d), _F32)], axis=0)
    mod = _modulation(cc, ada_w[0], ada_b[0][None, :])
    sh1, sc1, g1, sh2, sc2, g2 = [mod[:batch, i * d:(i + 1) * d][:, None, :] for i in range(6)]
    csh1, csc1 = [mod[batch:batch + 1, i * d:(i + 1) * d][:, None, :] for i in range(2)]

    w_in16 = w_in[0].astype(_BF16)
    n_in = w_in16.shape[1]
    o_k = wa_q
    o_v = o_k + wa_kv
    o_qb = o_v + wa_kv
    o_kb = o_qb + wb
    o_g = o_qb + 3 * wb

    g1n = norm1_g[0][None, :]
    h = _rmsnorm(xf, g1n, sc1, sh1, seq, _BF16)
    hc = _rmsnorm(ctxf, g1n, csc1, csh1, batch * ctx_len, _BF16)

    cos_t, sin_t = _rope_tables(seq)
    qk_cols = [(0, o_v)]
    tn_qk = _inproj_tile(qk_cols)
    assert wa_q % tn_qk == 0
    gain_q = jnp.broadcast_to(qnorm_a[0] * _QK_SCALE, (wa_q // tn_qk, 1, _HEAD_DIM))
    gain_k = jnp.broadcast_to(knorm_a[0], (wa_kv // tn_qk, 1, _HEAD_DIM))
    qk = _inproj(h, w_in16, qk_cols, "normrope", seq, jnp.concatenate([gain_q, gain_k], axis=0), cos_t, sin_t)
    pv_cols = [(o_v, o_g - o_v)]
    tn_pv = _inproj_tile(pv_cols)
    assert wa_kv % tn_pv == 0 and wb % tn_pv == 0
    pv_tile = jnp.arange((o_g - o_v) // tn_pv)
    is_qb = (pv_tile >= wa_kv // tn_pv) & (pv_tile < (wa_kv + wb) // tn_pv)
    pv_scale = jnp.broadcast_to(jnp.where(is_qb, _QK_SCALE, 1.0).astype(_F32)[:, None, None],
                                (pv_tile.shape[0], 1, _HEAD_DIM))
    pv = _inproj(h, w_in16, pv_cols, "scaled", seq, pv_scale)
    gates = _inproj(h, w_in16, [(o_g, n_in - o_g)], "sigmoid", seq)
    c_rows = batch * ctx_len
    kc_cols = [(o_k, wa_kv)]
    gain_kc = jnp.broadcast_to(knorm_a[0], (wa_kv // _inproj_tile(kc_cols), 1, _HEAD_DIM))
    ka_c = _inproj(hc, w_in16, kc_cols, "normrope", c_rows, gain_kc,
                   jnp.ones((c_rows, _HEAD_DIM), _F32), jnp.zeros((c_rows, _HEAD_DIM), _F32))
    pv_c = _inproj(hc, w_in16, [(o_v, wa_kv), (o_kb, 2 * wb)], "plain", c_rows)

    o_a, w_up16, w_down16 = _attn_a(qk, pv, ka_c, pv_c, batch, seq, ctx_len, kv_heads,
                                    casts=[w_up[0], w_down[0]])
    bias = _bias_tables(rpb_b[0], grid_rows)
    o_b, w_bra16, w_brb16, w_out16 = _attn_b(
        pv, pv_c, bias, batch, seq, ctx_len, heads_b,
        q_off=kv_heads, k_off=kv_heads + heads_b, v_off=kv_heads + 2 * heads_b,
        kc_off=kv_heads, vc_off=kv_heads + heads_b, casts=[w_br_a[0], w_br_b[0], w_out[0]])

    y = _merge(o_a, o_b, w_bra16, w_brb16, gates, d)
    x1 = _mm_residual(y, w_out16, xf, g1, seq,
                      (1024, 512, 256, 128), (1024, 512, 256, 128), "out_proj_residual")

    h2 = _rmsnorm(x1, norm2_g[0][None, :], sc2, sh2, seq, _BF16)
    t = _ffn_up(h2, w_up16, conv_w[0], conv_b[0][None, :], seq)
    x2 = _mm_residual(t, w_down16, x1, g2, seq,
                      (512, 256, 128), (512, 256, 128), "ffn_down_residual")

    out = _rmsnorm(x2, final_g[None, :], None, None, seq, _F32)
    return out.reshape(batch, seq, d)
```

```python
import functools
import math

import jax
import jax.numpy as jnp
from jax import lax
from jax.experimental import pallas as pl
from jax.experimental.pallas import tpu as pltpu

_HEAD_DIM = 128
_GRID_W = 64
_WIN_H = 8
_WIN_W = 16
_GROUPS_A = 4
_ROPE_THETA = 10000.0
_EPS = 1e-6
_BLK_ROWS = 4
_KEY_ROWS = _BLK_ROWS + _WIN_H
_SUBLANES = 8
_V7X_VMEM_BYTES = 64 * 1024 * 1024
_VMEM_CAP = _V7X_VMEM_BYTES - 6 * 1024 * 1024
_NEG_INF = float("-inf")
_LOG2_E = 1.4426950408889634
_QK_SCALE = _LOG2_E * _HEAD_DIM ** -0.5

_BF16 = jnp.bfloat16
_F32 = jnp.float32


def _params(vmem_bytes, semantics):
    limit = int(min(max(vmem_bytes, 32 * 1024 * 1024), _VMEM_CAP))
    return pltpu.CompilerParams(dimension_semantics=semantics, vmem_limit_bytes=limit)


def _dot(a, b):
    return jnp.dot(a, b, preferred_element_type=_F32)


def _dot_nt(a, b):
    return lax.dot_general(a, b, (((1,), (1,)), ((), ())), preferred_element_type=_F32)


def _pick_tile(n, candidates):
    for t in candidates:
        if n % t == 0:
            return t
    raise ValueError(f"no tile in {candidates} divides {n}")


def _cast_plan(weights, grid):
    n_steps = math.prod(grid)

    def step(*ids):
        s = ids[0]
        for n, i in zip(grid[1:], ids[1:]):
            s = s * n + i
        return s

    in_specs, out_specs, out_shapes, vmem = [], [], [], 0
    for w in weights:
        r, c = w.shape
        br = next(b for b in range(16, r + 1, 16) if r % b == 0 and r // b <= n_steps)
        last = r // br - 1

        def index(*ids, last=last):
            return jnp.minimum(step(*ids), last), 0

        in_specs.append(pl.BlockSpec((br, c), index))
        out_specs.append(pl.BlockSpec((br, c), index))
        out_shapes.append(jax.ShapeDtypeStruct((r, c), _BF16))
        vmem += 2 * br * c * (4 + 2) + br * c * 4
    return in_specs, out_specs, out_shapes, vmem


def _run_casts(srcs, dsts):
    for src, dst in zip(srcs, dsts):
        dst[...] = src[...].astype(dst.dtype)


def _mod_kernel(c_ref, w_ref, b_ref, o_ref):
    c = c_ref[...]
    s = (c * jax.nn.sigmoid(c)).astype(_BF16)
    o_ref[...] = _dot(s, w_ref[...].astype(_BF16)) + b_ref[...]


def _modulation(cc, ada_w, ada_b):
    rows, d = cc.shape
    n = ada_w.shape[1]
    tn = _pick_tile(n, (512, 256, 128))
    vmem = 2 * (d * tn * 4) + d * tn * 2 + 4 * rows * d * 4 + (4 << 20)
    return pl.pallas_call(
        _mod_kernel,
        grid=(n // tn,),
        in_specs=[
            pl.BlockSpec((rows, d), lambda j: (0, 0)),
            pl.BlockSpec((d, tn), lambda j: (0, j)),
            pl.BlockSpec((1, tn), lambda j: (0, j)),
        ],
        out_specs=pl.BlockSpec((rows, tn), lambda j: (0, j)),
        out_shape=jax.ShapeDtypeStruct((rows, n), _F32),
        compiler_params=_params(vmem, ("arbitrary",)),
        name="adaln_mod",
    )(cc, ada_w, ada_b)


def _norm_kernel(modulate, x_ref, g_ref, *rest):
    if modulate:
        sc_ref, sh_ref, o_ref = rest
    else:
        (o_ref,) = rest
    x = x_ref[...]
    ms = jnp.mean(x * x, axis=-1, keepdims=True)
    y = x * lax.rsqrt(ms + _EPS) * g_ref[...]
    if modulate:
        y = y * (1.0 + sc_ref[0]) + sh_ref[0]
    o_ref[...] = y.astype(o_ref.dtype)


def _rmsnorm(x, g, sc, sh, rows_per_group, out_dtype):
    m, d = x.shape
    tm = _pick_tile(rows_per_group, (512, 256, 128, 64, 32, 16, 8))
    tiles_per_group = rows_per_group // tm
    modulate = sc is not None
    in_specs = [pl.BlockSpec((tm, d), lambda i: (i, 0)), pl.BlockSpec((1, d), lambda i: (0, 0))]
    args = [x, g]
    if modulate:
        mod_spec = pl.BlockSpec((1, 1, d), lambda i: (i // tiles_per_group, 0, 0))
        in_specs += [mod_spec, mod_spec]
        args += [sc, sh]
    vmem = 2 * tm * d * 4 + 2 * tm * d * jnp.dtype(out_dtype).itemsize + 3 * tm * d * 4 + (4 << 20)
    return pl.pallas_call(
        functools.partial(_norm_kernel, modulate),
        grid=(m // tm,),
        in_specs=in_specs,
        out_specs=pl.BlockSpec((tm, d), lambda i: (i, 0)),
        out_shape=jax.ShapeDtypeStruct((m, d), out_dtype),
        compiler_params=_params(vmem, ("arbitrary",)),
        name="rmsnorm_mod" if modulate else "rmsnorm",
    )(*args)


def _rope_table_kernel(freq_ref, cos_ref, sin_ref):
    shape = cos_ref.shape
    t = lax.broadcasted_iota(jnp.int32, shape, 0)
    lane = lax.broadcasted_iota(jnp.int32, shape, 1)
    row = lax.shift_right_logical(t, _GRID_W.bit_length() - 1)
    col = jnp.bitwise_and(t, _GRID_W - 1)
    pos = jnp.where(lane < _HEAD_DIM // 2, row, col).astype(_F32)
    ang = pos * freq_ref[...]
    first_half = jnp.bitwise_and(lane, _HEAD_DIM // 2 - 1) < _HEAD_DIM // 4
    cos_ref[...] = jnp.cos(ang)
    sin_ref[...] = jnp.where(first_half, -jnp.sin(ang), jnp.sin(ang))


def _rope_tables(seq):
    r = _HEAD_DIM // 4
    freqs = _ROPE_THETA ** (-jnp.arange(r, dtype=_F32) / r)
    freq_lanes = jnp.tile(freqs, _HEAD_DIM // r)[None, :]
    out = jax.ShapeDtypeStruct((seq, _HEAD_DIM), _F32)
    return pl.pallas_call(
        _rope_table_kernel,
        out_shape=(out, out),
        name="rope_tables",
    )(freq_lanes)


def _inproj_kernel(kind, cm, h_ref, w_ref, *rest):
    if kind == "normrope":
        gain_ref, cos_ref, sin_ref, o_ref = rest
    elif kind == "scaled":
        gain_ref, o_ref = rest
    else:
        (o_ref,) = rest
    rows, tn = o_ref.shape
    w = w_ref[...]
    if kind in ("normrope", "scaled"):
        gain = gain_ref[0]
    if kind == "normrope":
        lane = lax.broadcasted_iota(jnp.int32, (cm, _HEAD_DIM), 1)
        first_half = jnp.bitwise_and(lane, _HEAD_DIM // 2 - 1) < _HEAD_DIM // 4
    for c in range(rows // cm):
        rs = slice(c * cm, (c + 1) * cm)
        acc = _dot(h_ref[rs, :], w)
        if kind == "plain":
            o_ref[rs, :] = acc.astype(o_ref.dtype)
        elif kind == "scaled":
            o_ref[rs, :] = (acc * gain[:, :1]).astype(o_ref.dtype)
        elif kind == "sigmoid":
            o_ref[rs, :] = jax.nn.sigmoid(acc).astype(o_ref.dtype)
        else:
            cos = cos_ref[rs, :]
            sin = sin_ref[rs, :]
            for hc in range(tn // _HEAD_DIM):
                cs = slice(hc * _HEAD_DIM, (hc + 1) * _HEAD_DIM)
                xh = acc[:, cs]
                ms = jnp.mean(xh * xh, axis=-1, keepdims=True)
                y = xh * lax.rsqrt(ms + _EPS) * gain
                partner = jnp.where(
                    first_half,
                    pltpu.roll(y, _HEAD_DIM - _HEAD_DIM // 4, axis=1),
                    pltpu.roll(y, _HEAD_DIM // 4, axis=1),
                )
                o_ref[rs, cs] = (y * cos + partner * sin).astype(o_ref.dtype)


def _inproj_tile(col_ranges):
    edges = [v for start_width in col_ranges for v in start_width]
    return _pick_tile(functools.reduce(math.gcd, edges), (512, 256, 128))


def _inproj(h, w, col_ranges, kind, slab_rows, gains=None, cos=None, sin=None):
    m, d = h.shape
    n = sum(width for _, width in col_ranges)
    tn = _inproj_tile(col_ranges)
    cm = _pick_tile(slab_rows, (512, 256, 128, 64, 32, 16))

    def w_block(j):
        blk, first = None, 0
        for start, width in col_ranges:
            here = start // tn + (j - first)
            blk = here if blk is None else jnp.where(j >= first, here, blk)
            first += width // tn
        return blk

    slab_mode = {"pipeline_mode": pl.Buffered(1)} if m == slab_rows else {}
    in_specs = [
        pl.BlockSpec((slab_rows, d), lambda b, j: (b, 0), **slab_mode),
        pl.BlockSpec((d, tn), lambda b, j: (0, w_block(j))),
    ]
    args = [h, w]
    vmem = 2 * slab_rows * d * 2 + 2 * d * tn * 2 + 2 * slab_rows * tn * 2 + 6 * cm * tn * 4 + (6 << 20)
    if kind in ("normrope", "scaled"):
        in_specs.append(pl.BlockSpec((1, 1, _HEAD_DIM), lambda b, j: (j, 0, 0)))
        args.append(gains)
    if kind == "normrope":
        period = cos.shape[0] // slab_rows
        table_spec = pl.BlockSpec((slab_rows, _HEAD_DIM), lambda b, j: (b % period, 0), **slab_mode)
        in_specs += [table_spec, table_spec]
        args += [cos, sin]
        vmem += 4 * slab_rows * _HEAD_DIM * 4
    return pl.pallas_call(
        functools.partial(_inproj_kernel, kind, cm),
        grid=(m // slab_rows, n // tn),
        in_specs=in_specs,
        out_specs=pl.BlockSpec((slab_rows, tn), lambda b, j: (b, j)),
        out_shape=jax.ShapeDtypeStruct((m, n), _BF16),
        compiler_params=_params(vmem, ("arbitrary", "arbitrary")),
        name="inproj_" + kind,
    )(*args)


def _softmax_pv(s_ref, p_ref, n_ctx, vc, v):
    s = s_ref[...]
    m = jnp.max(s, axis=-1, keepdims=True)
    p = jnp.exp2(s - m)
    denom = jnp.sum(p, axis=-1, keepdims=True)
    p_ref[...] = p.astype(p_ref.dtype)
    o = _dot(p_ref[:, :n_ctx], vc) + _dot(p_ref[:, n_ctx:], v)
    return o / denom


def _attn_a_kernel(n_cast, q_ref, k_ref, v_ref, kc_ref, vc_ref, *rest):
    cast_srcs, o_ref, cast_dsts = rest[:n_cast], rest[n_cast], rest[n_cast + 1:2 * n_cast + 1]
    s_ref, p_ref = rest[2 * n_cast + 1:]
    _run_casts(cast_srcs, cast_dsts)
    n_ctx = kc_ref.shape[0]
    k = k_ref[...]
    v = v_ref[...]
    kc = kc_ref[...]
    vc = vc_ref[...]

    ts = s_ref.shape[1]
    stages = [(slice(t * ts, (t + 1) * ts), slice(g * _HEAD_DIM, (g + 1) * _HEAD_DIM))
              for t in range(q_ref.shape[0] // ts) for g in range(_GROUPS_A)]

    def scores(n):
        q = q_ref[stages[n]]
        s_ref[n % 2, :, :n_ctx] = _dot_nt(q, kc)
        s_ref[n % 2, :, n_ctx:] = _dot_nt(q, k)

    def finish(n):
        o = _softmax_pv(s_ref.at[n % 2], p_ref.at[n % 2], n_ctx, vc, v)
        o_ref[stages[n]] = o.astype(o_ref.dtype)

    scores(0)
    for n in range(len(stages)):
        if n + 1 < len(stages):
            scores(n + 1)
        finish(n)


def _attn_a(qk, pv, ka_c, pv_c, batch, seq, ctx_len, kv_heads, casts):
    wq = kv_heads * _GROUPS_A * _HEAD_DIM
    gw = _GROUPS_A * _HEAD_DIM
    tq = _pick_tile(seq, (512, 256, 128))
    ts = min(tq, 256)
    nq = seq // tq
    koff = wq // _HEAD_DIM
    n_keys = seq + ctx_len
    vmem = (2 * tq * gw * 2 * 2 + 4 * seq * _HEAD_DIM * 2 + 4 * ctx_len * _HEAD_DIM * 2
            + 2 * ts * n_keys * 6 + 3 * ts * n_keys * 4 + (6 << 20))
    grid = (batch, kv_heads, nq)
    cast_in, cast_out, cast_shapes, cast_vmem = _cast_plan(casts, grid)
    return pl.pallas_call(
        functools.partial(_attn_a_kernel, len(casts)),
        grid=grid,
        in_specs=[
            pl.BlockSpec((tq, gw), lambda b, h, i: (b * nq + i, h)),
            pl.BlockSpec((seq, _HEAD_DIM), lambda b, h, i: (b, koff + h)),
            pl.BlockSpec((seq, _HEAD_DIM), lambda b, h, i: (b, h)),
            pl.BlockSpec((ctx_len, _HEAD_DIM), lambda b, h, i: (b, h)),
            pl.BlockSpec((ctx_len, _HEAD_DIM), lambda b, h, i: (b, h)),
        ] + cast_in,
        out_specs=[pl.BlockSpec((tq, gw), lambda b, h, i: (b * nq + i, h))] + cast_out,
        out_shape=[jax.ShapeDtypeStruct((batch * seq, wq), _BF16)] + cast_shapes,
        scratch_shapes=[pltpu.VMEM((2, ts, n_keys), _F32), pltpu.VMEM((2, ts, n_keys), _BF16)],
        compiler_params=_params(vmem + cast_vmem, ("arbitrary", "arbitrary", "arbitrary")),
        name="attn_gqa",
    )(qk, qk, pv, ka_c, pv_c, *casts)


def _window_start_row(blk, grid_rows):
    lo = _BLK_ROWS * blk - _WIN_H // 2
    if isinstance(blk, int):
        return min(max(lo, 0), grid_rows - _KEY_ROWS)
    return jnp.clip(lo, 0, grid_rows - _KEY_ROWS)


def _bias_table_kernel(grid_rows, rpb_ref, o_ref):
    h = pl.program_id(0)
    n_dr = 2 * _WIN_H - 1
    n_dc = 2 * _WIN_W - 1
    shape = (_GRID_W, 2 * _GRID_W)
    qc = lax.broadcasted_iota(jnp.int32, shape, 0)
    lane = lax.broadcasted_iota(jnp.int32, shape, 1)
    col_start = jnp.clip(qc - _WIN_W // 2, 0, _GRID_W - _WIN_W)
    neg = jnp.full(shape, _NEG_INF, _F32)

    def side(kc, on_side):
        ok = on_side & (kc >= col_start) & (kc < col_start + _WIN_W)
        return kc - qc + (_WIN_W - 1), ok

    diff_l, ok_l = side(lane, lane < _GRID_W)
    diff_r, ok_r = side(lane - _GRID_W, lane >= _GRID_W)
    left, right = [], []
    for dr in range(n_dr):
        acc_l, acc_r = neg, neg
        for dc in range(n_dc):
            val = rpb_ref[(h * n_dr + dr) * n_dc + dc] * _LOG2_E
            acc_l = jnp.where(ok_l & (diff_l == dc), val, acc_l)
            acc_r = jnp.where(ok_r & (diff_r == dc), val, acc_r)
        left.append(acc_l)
        right.append(acc_r)

    n_blk = grid_rows // _BLK_ROWS
    for variant, blk in enumerate((0, 1, n_blk - 1)):
        ws = _window_start_row(blk, grid_rows)
        for a in range(_BLK_ROWS):
            qr = _BLK_ROWS * blk + a
            rs = min(max(qr - _WIN_H // 2, 0), grid_rows - _WIN_H)
            for jp in range(_KEY_ROWS // 2):
                piece = None
                for half, table in ((0, left), (1, right)):
                    kr = ws + 2 * jp + half
                    if rs <= kr < rs + _WIN_H:
                        blk_bias = table[kr - qr + _WIN_H - 1]
                        piece = blk_bias if piece is None else jnp.maximum(piece, blk_bias)
                if piece is None:
                    piece = neg
                o_ref[0, variant, a * _GRID_W:(a + 1) * _GRID_W,
                      jp * 2 * _GRID_W:(jp + 1) * 2 * _GRID_W] = piece


def _bias_tables(rpb, grid_rows):
    heads = rpb.shape[0]
    blk = (1, 3, _BLK_ROWS * _GRID_W, _KEY_ROWS * _GRID_W)
    return pl.pallas_call(
        functools.partial(_bias_table_kernel, grid_rows),
        grid=(heads,),
        in_specs=[pl.BlockSpec(memory_space=pltpu.SMEM)],
        out_specs=pl.BlockSpec(blk, lambda h: (h, 0, 0, 0)),
        out_shape=jax.ShapeDtypeStruct((heads,) + blk[1:], _F32),
        compiler_params=_params(3 * blk[1] * blk[2] * blk[3] * 4 + (8 << 20), ("arbitrary",)),
        name="nbr_bias_tables",
    )(rpb.reshape(-1))


def _attn_b_kernel(grid_rows, n_cast, q_ref, k_ref, v_ref, kc_ref, vc_ref, bias_ref, *rest):
    cast_srcs, o_ref, cast_dsts = rest[:n_cast], rest[n_cast], rest[n_cast + 1:2 * n_cast + 1]
    s_ref, p_ref = rest[2 * n_cast + 1:]
    _run_casts(cast_srcs, cast_dsts)
    n_ctx = kc_ref.shape[0]
    n_blk = grid_rows // _BLK_ROWS
    tq = _BLK_ROWS * _GRID_W
    span = _KEY_ROWS * _GRID_W
    kc = kc_ref[...]
    vc = vc_ref[...]

    def window(i):
        start = _window_start_row(i, grid_rows) * _GRID_W
        return slice(start, start + span)

    def scores(i):
        q = q_ref[i * tq:(i + 1) * tq, :]
        variant = 0 if i == 0 else (2 if i == n_blk - 1 else 1)
        s_ref[i % 2, :, :n_ctx] = _dot_nt(q, kc)
        s_ref[i % 2, :, n_ctx:] = _dot_nt(q, k_ref[window(i), :]) + bias_ref[0, variant]

    def finish(i):
        o = _softmax_pv(s_ref.at[i % 2], p_ref.at[i % 2], n_ctx, vc, v_ref[window(i), :])
        o_ref[i * tq:(i + 1) * tq, :] = o.astype(o_ref.dtype)

    scores(0)
    for i in range(n_blk):
        if i + 1 < n_blk:
            scores(i + 1)
        finish(i)


def _attn_b(pv, pv_c, bias, batch, seq, ctx_len, heads, q_off, k_off, v_off, kc_off, vc_off, casts):
    grid_rows = seq // _GRID_W
    tq = _BLK_ROWS * _GRID_W
    span = _KEY_ROWS * _GRID_W
    n_keys = span + ctx_len
    vmem = (4 * 2 * seq * _HEAD_DIM * 2 + 4 * ctx_len * _HEAD_DIM * 2
            + 2 * 3 * tq * span * 4 + 2 * tq * n_keys * 6 + 3 * tq * n_keys * 4 + (6 << 20))
    grid = (heads, batch)
    cast_in, cast_out, cast_shapes, cast_vmem = _cast_plan(casts, grid)
    return pl.pallas_call(
        functools.partial(_attn_b_kernel, grid_rows, len(casts)),
        grid=grid,
        in_specs=[
            pl.BlockSpec((seq, _HEAD_DIM), lambda h, b: (b, q_off + h)),
            pl.BlockSpec((seq, _HEAD_DIM), lambda h, b: (b, k_off + h)),
            pl.BlockSpec((seq, _HEAD_DIM), lambda h, b: (b, v_off + h)),
            pl.BlockSpec((ctx_len, _HEAD_DIM), lambda h, b: (b, kc_off + h)),
            pl.BlockSpec((ctx_len, _HEAD_DIM), lambda h, b: (b, vc_off + h)),
            pl.BlockSpec((1, 3, tq, span), lambda h, b: (h, 0, 0, 0)),
        ] + cast_in,
        out_specs=[pl.BlockSpec((seq, _HEAD_DIM), lambda h, b: (b, h))] + cast_out,
        out_shape=[jax.ShapeDtypeStruct((batch * seq, heads * _HEAD_DIM), _BF16)] + cast_shapes,
        scratch_shapes=[pltpu.VMEM((2, tq, n_keys), _F32), pltpu.VMEM((2, tq, n_keys), _BF16)],
        compiler_params=_params(vmem + cast_vmem, ("arbitrary", "arbitrary")),
        name="attn_nbr",
    )(pv, pv, pv, pv_c, pv_c, bias, *casts)


def _merge_kernel(oa_ref, ob_ref, wa_ref, wb_ref, ga_ref, gb_ref, y_ref):
    ya = _dot(oa_ref[...], wa_ref[...])
    yb = _dot(ob_ref[...], wb_ref[...])
    y = ga_ref[...].astype(_F32) * ya + gb_ref[...].astype(_F32) * yb
    y_ref[...] = y.astype(y_ref.dtype)


def _merge(o_a, o_b, w_a, w_b, gates, d_model):
    m, ka = o_a.shape
    kb = o_b.shape[1]
    tm = _pick_tile(m, (1024, 512, 256, 128))
    tn = _pick_tile(d_model, (1024, 512, 256, 128))
    nj = d_model // tn
    vmem = 2 * (tm * (ka + kb) * 2 + (ka + kb) * tn * 2 + 3 * tm * tn * 2) + 4 * tm * tn * 4 + (6 << 20)
    return pl.pallas_call(
        _merge_kernel,
        grid=(m // tm, nj),
        in_specs=[
            pl.BlockSpec((tm, ka), lambda i, j: (i, 0)),
            pl.BlockSpec((tm, kb), lambda i, j: (i, 0)),
            pl.BlockSpec((ka, tn), lambda i, j: (0, j)),
            pl.BlockSpec((kb, tn), lambda i, j: (0, j)),
            pl.BlockSpec((tm, tn), lambda i, j: (i, j)),
            pl.BlockSpec((tm, tn), lambda i, j: (i, nj + j)),
        ],
        out_specs=pl.BlockSpec((tm, tn), lambda i, j: (i, j)),
        out_shape=jax.ShapeDtypeStruct((m, d_model), _BF16),
        compiler_params=_params(vmem, ("arbitrary", "arbitrary")),
        name="branch_merge",
    )(o_a, o_b, w_a, w_b, gates, gates)


def _mm_residual_kernel(a_ref, w_ref, res_ref, gate_ref, o_ref):
    acc = _dot(a_ref[...], w_ref[...])
    o_ref[...] = res_ref[...] + gate_ref[0] * acc


def _mm_residual(a, w, res, gate, rows_per_group, tm_cands, tn_cands, name):
    m, k = a.shape
    n = w.shape[1]
    tm = _pick_tile(rows_per_group, tm_cands)
    tn = _pick_tile(n, tn_cands)
    tiles_per_group = rows_per_group // tm
    vmem = 2 * (tm * k * 2 + k * tn * 2 + 2 * tm * tn * 4) + 2 * tm * tn * 4 + (6 << 20)
    return pl.pallas_call(
        _mm_residual_kernel,
        grid=(m // tm, n // tn),
        in_specs=[
            pl.BlockSpec((tm, k), lambda i, j: (i, 0)),
            pl.BlockSpec((k, tn), lambda i, j: (0, j)),
            pl.BlockSpec((tm, tn), lambda i, j: (i, j)),
            pl.BlockSpec((1, 1, tn), lambda i, j: (i // tiles_per_group, 0, j)),
        ],
        out_specs=pl.BlockSpec((tm, tn), lambda i, j: (i, j)),
        out_shape=jax.ShapeDtypeStruct((m, n), _F32),
        compiler_params=_params(vmem, ("arbitrary", "arbitrary")),
        name=name,
    )(a, w, res, gate)


def _ffn_up_kernel(cm, h_ref, wg_ref, wv_ref, cwg_ref, cwv_ref, cbg_ref, cbv_ref, o_ref):
    rows, tn = o_ref.shape
    n_chunks = rows // cm
    wg = wg_ref[...]
    wv = wv_ref[...]
    ug, uv = [], []
    for c in range(n_chunks):
        hs = h_ref[c * cm:(c + 1) * cm, :]
        ug.append(_dot(hs, wg))
        uv.append(_dot(hs, wv))
    groups = cm // _SUBLANES
    sub = lax.broadcasted_iota(jnp.int32, (groups, _SUBLANES, tn), 1)
    zero_group = jnp.zeros((1, _SUBLANES, tn), _F32)

    def conv(u, c, cw_ref, cb_ref):
        u3 = [x.reshape(groups, _SUBLANES, tn) for x in u]
        before = u3[c - 1][groups - 1:] if c > 0 else zero_group
        after = u3[c + 1][:1] if c + 1 < n_chunks else zero_group
        above = jnp.concatenate([before, u3[c][:groups - 1]], axis=0)
        below = jnp.concatenate([u3[c][1:], after], axis=0)
        prev = pltpu.roll(jnp.where(sub == _SUBLANES - 1, above, u3[c]), 1, axis=1)
        nxt = pltpu.roll(jnp.where(sub == 0, below, u3[c]), _SUBLANES - 1, axis=1)
        out = prev * cw_ref[0:1, :] + u3[c] * cw_ref[1:2, :] + nxt * cw_ref[2:3, :] + cb_ref[...]
        return out.reshape(cm, tn)

    for c in range(n_chunks):
        gate = conv(ug, c, cwg_ref, cbg_ref)
        val = conv(uv, c, cwv_ref, cbv_ref)
        o_ref[c * cm:(c + 1) * cm, :] = (gate * jax.nn.sigmoid(gate) * val).astype(o_ref.dtype)


def _ffn_up(h2, w_up, conv_w, conv_b, seq):
    m, d = h2.shape
    d_ff = w_up.shape[1] // 2
    tn = _pick_tile(d_ff, (256, 128))
    nj = d_ff // tn
    cm = _pick_tile(seq, (512, 256, 128, 64, 32, 16))
    taps = conv_w.shape[0]
    vmem = seq * d * 2 + 2 * 2 * d * tn * 2 + 2 * seq * tn * 2 + 6 * seq * tn * 4 + (6 << 20)
    return pl.pallas_call(
        functools.partial(_ffn_up_kernel, cm),
        grid=(m // seq, nj),
        in_specs=[
            pl.BlockSpec((seq, d), lambda b, j: (b, 0), pipeline_mode=pl.Buffered(1)),
            pl.BlockSpec((d, tn), lambda b, j: (0, j)),
            pl.BlockSpec((d, tn), lambda b, j: (0, nj + j)),
            pl.BlockSpec((taps, tn), lambda b, j: (0, j)),
            pl.BlockSpec((taps, tn), lambda b, j: (0, nj + j)),
            pl.BlockSpec((1, tn), lambda b, j: (0, j)),
            pl.BlockSpec((1, tn), lambda b, j: (0, nj + j)),
        ],
        out_specs=pl.BlockSpec((seq, tn), lambda b, j: (b, j)),
        out_shape=jax.ShapeDtypeStruct((m, d_ff), _BF16),
        compiler_params=_params(vmem, ("arbitrary", "arbitrary")),
        name="ffn_up_conv_gate",
    )(h2, w_up, w_up, conv_w, conv_w, conv_b, conv_b)


def kernel(x, c, ctx, c_ctx, ada_w, ada_b, norm1_g, w_in, qnorm_a, knorm_a, rpb_b,
           w_br_a, w_br_b, w_out, norm2_g, w_up, conv_w, conv_b, w_down, final_g):
    batch, seq, d = x.shape
    ctx_len = ctx.shape[1]
    depth = ada_w.shape[0]
    wa_q = w_br_a.shape[1]
    wb = w_br_b.shape[1]
    wa_kv = (w_in.shape[2] - wa_q - 3 * wb - 2 * d) // 2
    kv_heads = wa_kv // _HEAD_DIM
    heads_b = wb // _HEAD_DIM
    assert depth == 1, "the context-stream update between layers is not implemented"
    assert wa_q == kv_heads * _GROUPS_A * _HEAD_DIM and seq % (_BLK_ROWS * _GRID_W) == 0
    assert seq // _GRID_W >= _KEY_ROWS and qnorm_a.shape[-1] == _HEAD_DIM
    grid_rows = seq // _GRID_W
    m = batch * seq
    xf = x.reshape(m, d)
    ctxf = ctx.reshape(batch * ctx_len, d)

    pad_rows = -(batch + 1) % 16
    cc = jnp.concatenate([c, c_ctx[None, :], jnp.zeros((pad_rows, d), _F32)], axis=0)
    mod = _modulation(cc, ada_w[0], ada_b[0][None, :])
    sh1, sc1, g1, sh2, sc2, g2 = [mod[:batch, i * d:(i + 1) * d][:, None, :] for i in range(6)]
    csh1, csc1 = [mod[batch:batch + 1, i * d:(i + 1) * d][:, None, :] for i in range(2)]

    w_in16 = w_in[0].astype(_BF16)
    n_in = w_in16.shape[1]
    o_k = wa_q
    o_v = o_k + wa_kv
    o_qb = o_v + wa_kv
    o_kb = o_qb + wb
    o_g = o_qb + 3 * wb

    g1n = norm1_g[0][None, :]
    h = _rmsnorm(xf, g1n, sc1, sh1, seq, _BF16)
    hc = _rmsnorm(ctxf, g1n, csc1, csh1, batch * ctx_len, _BF16)

    cos_t, sin_t = _rope_tables(seq)
    slab = seq // 2
    qk_cols = [(0, o_v)]
    tn_qk = _inproj_tile(qk_cols)
    assert wa_q % tn_qk == 0
    gain_q = jnp.broadcast_to(qnorm_a[0] * _QK_SCALE, (wa_q // tn_qk, 1, _HEAD_DIM))
    gain_k = jnp.broadcast_to(knorm_a[0], (wa_kv // tn_qk, 1, _HEAD_DIM))
    qk = _inproj(h, w_in16, qk_cols, "normrope", slab, jnp.concatenate([gain_q, gain_k], axis=0), cos_t, sin_t)
    pv_cols = [(o_v, o_g - o_v)]
    tn_pv = _inproj_tile(pv_cols)
    assert wa_kv % tn_pv == 0 and wb % tn_pv == 0
    pv_tile = jnp.arange((o_g - o_v) // tn_pv)
    is_qb = (pv_tile >= wa_kv // tn_pv) & (pv_tile < (wa_kv + wb) // tn_pv)
    pv_scale = jnp.broadcast_to(jnp.where(is_qb, _QK_SCALE, 1.0).astype(_F32)[:, None, None],
                                (pv_tile.shape[0], 1, _HEAD_DIM))
    pv = _inproj(h, w_in16, pv_cols, "scaled", slab, pv_scale)
    gates = _inproj(h, w_in16, [(o_g, n_in - o_g)], "sigmoid", slab)
    c_rows = batch * ctx_len
    kc_cols = [(o_k, wa_kv)]
    gain_kc = jnp.broadcast_to(knorm_a[0], (wa_kv // _inproj_tile(kc_cols), 1, _HEAD_DIM))
    ka_c = _inproj(hc, w_in16, kc_cols, "normrope", c_rows, gain_kc,
                   jnp.ones((c_rows, _HEAD_DIM), _F32), jnp.zeros((c_rows, _HEAD_DIM), _F32))
    pv_c = _inproj(hc, w_in16, [(o_v, wa_kv), (o_kb, 2 * wb)], "plain", c_rows)

    o_a, w_up16, w_down16 = _attn_a(qk, pv, ka_c, pv_c, batch, seq, ctx_len, kv_heads,
                                    casts=[w_up[0], w_down[0]])
    bias = _bias_tables(rpb_b[0], grid_rows)
    o_b, w_bra16, w_brb16, w_out16 = _attn_b(
        pv, pv_c, bias, batch, seq, ctx_len, heads_b,
        q_off=kv_heads, k_off=kv_heads + heads_b, v_off=kv_heads + 2 * heads_b,
        kc_off=kv_heads, vc_off=kv_heads + heads_b, casts=[w_br_a[0], w_br_b[0], w_out[0]])

    y = _merge(o_a, o_b, w_bra16, w_brb16, gates, d)
    x1 = _mm_residual(y, w_out16, xf, g1, seq,
                      (1024, 512, 256, 128), (1024, 512, 256, 128), "out_proj_residual")

    h2 = _rmsnorm(x1, norm2_g[0][None, :], sc2, sh2, seq, _BF16)
    t = _ffn_up(h2, w_up16, conv_w[0], conv_b[0][None, :], seq)
    x2 = _mm_residual(t, w_down16, x1, g2, seq,
                      (512, 256, 128), (512, 256, 128), "ffn_down_residual")

    out = _rmsnorm(x2, final_g[None, :], None, None, seq, _F32)
    return out.reshape(batch, seq, d)
```
